```python
import jax, jax.numpy as jnp
from jax import lax
import numpy as np

D_MODEL = 2048
BATCH = 2
SEQ = 4096
DEPTH = 2

N_MIXERS = 2
CHUNK = 128
GMLP_EXPAND = 2
GMLP_WIDTH = GMLP_EXPAND * D_MODEL
GMLP_GROUPS = 16
GMLP_GROUP_DIM = GMLP_WIDTH // GMLP_GROUPS
SB_HEAD_DIM = 128
SB_HEADS = D_MODEL // SB_HEAD_DIM
SB_WIDTH = SB_HEADS * SB_HEAD_DIM
Q_BLOCK = 128
N_A = (DEPTH + 1) // 2
N_B = DEPTH // 2
EPS = 1e-6

kernel_name = "hybrid_gmlp_stickbreaking_trunk"


def rms_norm(x, g):
    xf = x.astype(jnp.float32)
    y = xf * lax.rsqrt(jnp.mean(xf * xf, axis=-1, keepdims=True) + EPS)
    return (y * g.astype(jnp.float32)).astype(x.dtype)


def gmlp_branch(xn, w_in, v_g, w_s, b_s, w_out):
    B, S, _ = xn.shape
    proj = xn @ w_in
    uv = jax.nn.gelu(proj[..., :2 * GMLP_WIDTH])
    zg = proj[..., 2 * GMLP_WIDTH:]
    u, v = jnp.split(uv, 2, axis=-1)
    v = rms_norm(v, v_g)
    vc = v.reshape(B, S // CHUNK, CHUNK, GMLP_GROUPS, GMLP_GROUP_DIM)
    causal = jnp.tril(jnp.ones((CHUNK, CHUNK), dtype=w_s.dtype))
    ws = w_s * causal[None]
    mixed = jnp.einsum('gts,bnsgc->bntgc', ws, vc) + jnp.transpose(b_s)[None, None, :, :, None]
    mixed = mixed.reshape(B, S, GMLP_WIDTH)
    y = u * mixed * jax.nn.silu(zg)
    return y @ w_out


def stick_breaking_attention(q, k, v):
    B, S, H, dh = q.shape
    nb = S // Q_BLOCK
    qh = jnp.transpose(q, (0, 2, 1, 3)) * (dh ** -0.5)
    kh = jnp.transpose(k, (0, 2, 1, 3))
    vh = jnp.transpose(v, (0, 2, 1, 3))
    qb = jnp.transpose(qh.reshape(B, H, nb, Q_BLOCK, dh), (2, 0, 1, 3, 4))
    t0 = jnp.arange(nb, dtype=jnp.int32) * Q_BLOCK
    s_idx = jnp.arange(S, dtype=jnp.int32)[None, :]

    def block(args):
        qblk, start = args
        z = jnp.einsum('bhqd,bhkd->bhqk', qblk, kh).astype(jnp.float32)
        t_idx = start + jnp.arange(Q_BLOCK, dtype=jnp.int32)[:, None]
        mask = s_idx < t_idx
        log_fail = jnp.where(mask, jax.nn.log_sigmoid(-z), 0.0)
        tail = lax.cumsum(log_fail, axis=3, reverse=True) - log_fail
        log_a = jax.nn.log_sigmoid(z) + tail
        a = jnp.where(mask, jnp.exp(log_a), 0.0)
        return jnp.einsum('bhqk,bhkd->bhqd', a.astype(vh.dtype), vh)

    ob = lax.map(block, (qb, t0))
    return jnp.transpose(ob, (1, 0, 3, 2, 4)).reshape(B, S, H, dh)


def stick_breaking_branch(xn, w_in, w_out):
    B, S, _ = xn.shape
    proj = xn @ w_in
    q, k, v, zg = jnp.split(proj, 4, axis=-1)
    shp = (B, S, SB_HEADS, SB_HEAD_DIM)
    o = stick_breaking_attention(q.reshape(shp), k.reshape(shp), v.reshape(shp))
    o = o.reshape(B, S, SB_WIDTH) * jax.nn.silu(zg)
    return o @ w_out


def setup_inputs(seed: int = 0) -> dict:
    key = jax.random.key(seed)
    ks = jax.random.split(key, 12)
    f32 = jnp.float32
    x = jax.random.normal(ks[0], (BATCH, SEQ, D_MODEL), f32)
    norm_g = 1.0 + 0.02 * jax.random.normal(ks[1], (DEPTH, D_MODEL), f32)
    a_w_in = jax.random.normal(ks[2], (N_A, D_MODEL, 3 * GMLP_WIDTH), f32) * D_MODEL ** -0.5
    a_v_norm_g = 1.0 + 0.02 * jax.random.normal(ks[3], (N_A, GMLP_WIDTH), f32)
    a_w_s = jax.random.normal(ks[4], (N_A, GMLP_GROUPS, CHUNK, CHUNK), f32) * (0.5 * CHUNK ** -0.5)
    a_b_s = 1.0 + 0.1 * jax.random.normal(ks[5], (N_A, GMLP_GROUPS, CHUNK), f32)
    a_w_out = jax.random.normal(ks[6], (N_A, GMLP_WIDTH, D_MODEL), f32) * GMLP_WIDTH ** -0.5
    b_w_in = jax.random.normal(ks[7], (N_B, D_MODEL, 4 * SB_WIDTH), f32) * D_MODEL ** -0.5
    b_w_out = jax.random.normal(ks[8], (N_B, SB_WIDTH, D_MODEL), f32) * SB_WIDTH ** -0.5
    final_g = 1.0 + 0.02 * jax.random.normal(ks[9], (D_MODEL,), f32)
    return {"x": x, "norm_g": norm_g, "a_w_in": a_w_in, "a_v_norm_g": a_v_norm_g,
            "a_w_s": a_w_s, "a_b_s": a_b_s, "a_w_out": a_w_out,
            "b_w_in": b_w_in, "b_w_out": b_w_out, "final_g": final_g}


def reference(x, norm_g, a_w_in, a_v_norm_g, a_w_s, a_b_s, a_w_out, b_w_in, b_w_out, final_g):
    h = x
    for i in range(DEPTH):
        hn = rms_norm(h, norm_g[i])
        j = i // N_MIXERS
        if i % N_MIXERS == 0:
            y = gmlp_branch(hn, a_w_in[j], a_v_norm_g[j], a_w_s[j], a_b_s[j], a_w_out[j])
        else:
            y = stick_breaking_branch(hn, b_w_in[j], b_w_out[j])
        h = h + y
    return rms_norm(h, final_g)
```

```python
import functools

import jax
import jax.numpy as jnp
from jax import lax
from jax.experimental import pallas as pl
from jax.experimental.pallas import tpu as pltpu

EPS = 1e-6
CHUNK = 128
GMLP_GROUPS = 16
SB_HEAD_DIM = 128
GELU_C = 0.7978845608028654

V7X_VMEM_LIMIT_BYTES = 56 * 1024 * 1024

BF16 = jnp.bfloat16
F32 = jnp.float32


def _gelu_tanh(x):
    return 0.5 * x * (1.0 + jnp.tanh(GELU_C * (x + 0.044715 * (x * x * x))))


def _silu(x):
    return 0.5 * x * (1.0 + jnp.tanh(0.5 * x))


def _rms_normalize(x_f32, gain_f32):
    ms = jnp.mean(x_f32 * x_f32, axis=-1, keepdims=True)
    return x_f32 * lax.rsqrt(ms + EPS) * gain_f32


def _params(semantics):
    return pltpu.CompilerParams(dimension_semantics=semantics,
                                vmem_limit_bytes=V7X_VMEM_LIMIT_BYTES)


def _gmlp_in_kernel(x_ref, g_ref, wu_ref, wv_ref, wg_ref, ug_ref, v_ref, rinv_ref,
                    xn_ref, ssq_ref, *, width):
    j = pl.program_id(1)

    @pl.when(j == 0)
    def _():
        xn_ref[...] = _rms_normalize(x_ref[...], g_ref[...]).astype(BF16)
        ssq_ref[...] = jnp.zeros_like(ssq_ref)

    xn = xn_ref[...]
    u = jnp.dot(xn, wu_ref[...], preferred_element_type=F32)
    v = jnp.dot(xn, wv_ref[...], preferred_element_type=F32)
    zg = jnp.dot(xn, wg_ref[...], preferred_element_type=F32)
    v = _gelu_tanh(v)
    ug_ref[...] = (_gelu_tanh(u) * _silu(zg)).astype(BF16)
    v_ref[...] = v.astype(BF16)
    ssq_ref[...] += jnp.sum(v * v, axis=-1, keepdims=True)

    @pl.when(j == pl.num_programs(1) - 1)
    def _():
        rinv_ref[...] = lax.rsqrt(ssq_ref[...] * (1.0 / width) + EPS)


def _gmlp_in(x2, gain, w_in_bf16, *, tm, tn):
    tokens, d_model = x2.shape
    width = w_in_bf16.shape[1] // 3
    nj = width // tn
    grid = (tokens // tm, nj)
    w_spec = lambda off: pl.BlockSpec((d_model, tn), lambda i, j, off=off: (0, j + off))
    return pl.pallas_call(
        functools.partial(_gmlp_in_kernel, width=width),
        grid=grid,
        in_specs=[
            pl.BlockSpec((tm, d_model), lambda i, j: (i, 0)),
            pl.BlockSpec((1, d_model), lambda i, j: (0, 0)),
            w_spec(0), w_spec(nj), w_spec(2 * nj),
        ],
        out_specs=[
            pl.BlockSpec((tm, tn), lambda i, j: (i, j)),
            pl.BlockSpec((tm, tn), lambda i, j: (i, j)),
            pl.BlockSpec((tm, 1), lambda i, j: (i, 0)),
        ],
        out_shape=[
            jax.ShapeDtypeStruct((tokens, width), BF16),
            jax.ShapeDtypeStruct((tokens, width), BF16),
            jax.ShapeDtypeStruct((tokens, 1), F32),
        ],
        scratch_shapes=[pltpu.VMEM((tm, d_model), BF16), pltpu.VMEM((tm, 1), F32)],
        compiler_params=_params(("parallel", "arbitrary")),
        name="gmlp_in_proj",
    )(x2, gain, w_in_bf16, w_in_bf16, w_in_bf16)


def _gmlp_out_kernel(x_ref, v_ref, ug_ref, rinv_ref, vg_ref, ws_ref, bs_ref, wo_ref, h_ref, *, tm):
    g = pl.program_id(1)

    @pl.when(g == 0)
    def _():
        h_ref[...] = x_ref[...]

    row = lax.broadcasted_iota(jnp.int32, (CHUNK, CHUNK), 0)
    col = lax.broadcasted_iota(jnp.int32, (CHUNK, CHUNK), 1)
    ws = jnp.where(row >= col, ws_ref[0], 0.0).astype(BF16)
    bias = bs_ref[0]
    vn = (v_ref[...].astype(F32) * rinv_ref[...] * vg_ref[...]).astype(BF16)
    ys = []
    for c in range(tm // CHUNK):
        rows = slice(c * CHUNK, (c + 1) * CHUNK)
        mixed = jnp.dot(ws, vn[rows], preferred_element_type=F32) + bias
        ys.append((ug_ref[rows, :].astype(F32) * mixed).astype(BF16))
    y = jnp.concatenate(ys, axis=0)
    h_ref[...] += jnp.dot(y, wo_ref[...], preferred_element_type=F32)


def _gmlp_out(x2, v, ug, rinv, v_gain, w_s, b_s3, w_out_bf16, *, tm):
    tokens, d_model = x2.shape
    width = v.shape[1]
    gd = width // GMLP_GROUPS
    grid = (tokens // tm, GMLP_GROUPS)
    return pl.pallas_call(
        functools.partial(_gmlp_out_kernel, tm=tm),
        grid=grid,
        in_specs=[
            pl.BlockSpec((tm, d_model), lambda i, g: (i, 0)),
            pl.BlockSpec((tm, gd), lambda i, g: (i, g)),
            pl.BlockSpec((tm, gd), lambda i, g: (i, g)),
            pl.BlockSpec((tm, 1), lambda i, g: (i, 0)),
            pl.BlockSpec((1, gd), lambda i, g: (0, g)),
            pl.BlockSpec((1, CHUNK, CHUNK), lambda i, g: (g, 0, 0)),
            pl.BlockSpec((1, CHUNK, 1), lambda i, g: (g, 0, 0)),
            pl.BlockSpec((gd, d_model), lambda i, g: (g, 0)),
        ],
        out_specs=pl.BlockSpec((tm, d_model), lambda i, g: (i, 0)),
        out_shape=jax.ShapeDtypeStruct((tokens, d_model), F32),
        compiler_params=_params(("parallel", "arbitrary")),
        name="gmlp_mix_out_proj",
    )(x2, v, ug, rinv, v_gain, w_s, b_s3, w_out_bf16)


def _sb_in_kernel(h_ref, g_ref, wq_ref, wk_ref, wv_ref, wg_ref, q_ref, k_ref, v_ref, sg_ref, hn_ref):
    j = pl.program_id(1)

    @pl.when(j == 0)
    def _():
        hn_ref[...] = _rms_normalize(h_ref[...], g_ref[...]).astype(BF16)

    hn = hn_ref[...]
    q = jnp.dot(hn, wq_ref[...], preferred_element_type=F32)
    q_ref[...] = (q * (SB_HEAD_DIM ** -0.5)).astype(BF16)
    k_ref[...] = jnp.dot(hn, wk_ref[...], preferred_element_type=F32).astype(BF16)
    v_ref[...] = jnp.dot(hn, wv_ref[...], preferred_element_type=F32).astype(BF16)
    sg_ref[...] = _silu(jnp.dot(hn, wg_ref[...], preferred_element_type=F32)).astype(BF16)


def _sb_in(h, gain, w_in_bf16, *, tm, tn):
    tokens, d_model = h.shape
    width = w_in_bf16.shape[1] // 4
    nj = width // tn
    grid = (tokens // tm, nj)
    w_spec = lambda off: pl.BlockSpec((d_model, tn), lambda i, j, off=off: (0, j + off))
    out_spec = pl.BlockSpec((tm, tn), lambda i, j: (i, j))
    out_shape = jax.ShapeDtypeStruct((tokens, width), BF16)
    return pl.pallas_call(
        _sb_in_kernel,
        grid=grid,
        in_specs=[
            pl.BlockSpec((tm, d_model), lambda i, j: (i, 0)),
            pl.BlockSpec((1, d_model), lambda i, j: (0, 0)),
            w_spec(0), w_spec(nj), w_spec(2 * nj), w_spec(3 * nj),
        ],
        out_specs=[out_spec] * 4,
        out_shape=[out_shape] * 4,
        scratch_shapes=[pltpu.VMEM((tm, d_model), BF16)],
        compiler_params=_params(("parallel", "arbitrary")),
        name="sb_in_proj",
    )(h, gain, w_in_bf16, w_in_bf16, w_in_bf16, w_in_bf16)


def _sb_attn_kernel(q_ref, k_ref, v_ref, sg_ref, o_ref, *, tq):
    qi = pl.program_id(2)
    q = q_ref[...]
    row = lax.broadcasted_iota(jnp.int32, (tq, tq), 0)
    col = lax.broadcasted_iota(jnp.int32, (tq, tq), 1)
    suffix = (row > col).astype(BF16)
    causal = col < row

    def block(kb, carry, acc, masked):
        keys = pl.ds(pl.multiple_of(kb * tq, tq), tq)
        z = lax.dot_general(q, k_ref[keys, :], (((1,), (1,)), ((), ())),
                            preferred_element_type=F32)
        softplus = jnp.maximum(z, 0.0) + jnp.log(1.0 + jnp.exp(-jnp.abs(z)))
        log_fail = -softplus
        log_beta = z - softplus
        if masked:
            log_fail = jnp.where(causal, log_fail, 0.0)
        tail = jnp.dot(log_fail.astype(BF16), suffix, preferred_element_type=F32)
        a = jnp.exp(log_beta + tail + carry)
        if masked:
            a = jnp.where(causal, a, 0.0)
        acc = acc + jnp.dot(a.astype(BF16), v_ref[keys, :], preferred_element_type=F32)
        carry = carry + jnp.sum(log_fail, axis=-1, keepdims=True)
        return carry, acc

    carry0 = jnp.zeros((tq, 1), F32)
    acc0 = jnp.zeros((tq, SB_HEAD_DIM), F32)
    carry, acc = block(qi, carry0, acc0, masked=True)

    def body(it, state):
        return block(qi - 1 - it, *state, masked=False)

    _, acc = lax.fori_loop(0, qi, body, (carry, acc))
    o_ref[...] = (acc * sg_ref[...].astype(F32)).astype(BF16)


def _sb_attn(q, k, v, sg, *, batch, seq, tq):
    tokens, width = q.shape
    heads = width // SB_HEAD_DIM
    nq = seq // tq
    grid = (batch, heads, nq)
    q_spec = pl.BlockSpec((tq, SB_HEAD_DIM), lambda b, h, i: (b * nq + i, h))
    kv_spec = pl.BlockSpec((seq, SB_HEAD_DIM), lambda b, h, i: (b, h))
    return pl.pallas_call(
        functools.partial(_sb_attn_kernel, tq=tq),
        grid=grid,
        in_specs=[q_spec, kv_spec, kv_spec, q_spec],
        out_specs=q_spec,
        out_shape=jax.ShapeDtypeStruct((tokens, width), BF16),
        compiler_params=_params(("parallel", "parallel", "arbitrary")),
        name="sb_attention",
    )(q, k, v, sg)


def _sb_out_kernel(h_ref, o_ref, wo_ref, g_ref, out_ref):
    h2 = h_ref[...] + jnp.dot(o_ref[...], wo_ref[...], preferred_element_type=F32)
    out_ref[...] = _rms_normalize(h2, g_ref[...])


def _sb_out(h, og, w_out_bf16, final_g, *, tm):
    tokens, d_model = h.shape
    width = og.shape[1]
    return pl.pallas_call(
        _sb_out_kernel,
        grid=(tokens // tm,),
        in_specs=[
            pl.BlockSpec((tm, d_model), lambda i: (i, 0)),
            pl.BlockSpec((tm, width), lambda i: (i, 0)),
            pl.BlockSpec((width, d_model), lambda i: (0, 0)),
            pl.BlockSpec((1, d_model), lambda i: (0, 0)),
        ],
        out_specs=pl.BlockSpec((tm, d_model), lambda i: (i, 0)),
        out_shape=jax.ShapeDtypeStruct((tokens, d_model), F32),
        compiler_params=_params(("parallel",)),
        name="sb_out_proj_norm",
    )(h, og, w_out_bf16, final_g)


def kernel(x, norm_g, a_w_in, a_v_norm_g, a_w_s, a_b_s, a_w_out, b_w_in, b_w_out, final_g):
    batch, seq, d_model = x.shape
    assert norm_g.shape[0] == 2 and a_w_in.shape[0] == 1 and b_w_in.shape[0] == 1
    x2 = x.reshape(batch * seq, d_model)
    ug, v, rinv = _gmlp_in(x2, norm_g[0][None, :], a_w_in[0].astype(BF16), tm=1024, tn=512)
    h1 = _gmlp_out(x2, v, ug, rinv, a_v_norm_g[0][None, :], a_w_s[0], a_b_s[0][:, :, None],
                   a_w_out[0].astype(BF16), tm=1024)
    q, k, v, sg = _sb_in(h1, norm_g[1][None, :], b_w_in[0].astype(BF16), tm=1024, tn=512)
    og = _sb_attn(q, k, v, sg, batch=batch, seq=seq, tq=256)
    out = _sb_out(h1, og, b_w_out[0].astype(BF16), final_g[None, :], tm=512)
    return out.reshape(batch, seq, d_model)
```

```python
import functools

import jax
import jax.numpy as jnp
from jax import lax
from jax.experimental import pallas as pl
from jax.experimental.pallas import tpu as pltpu

EPS = 1e-6
CHUNK = 128
GMLP_GROUPS = 16
SB_HEAD_DIM = 128
GELU_C = 0.7978845608028654
LOG2_E = 1.4426950408889634
ZERO_WEIGHT_LOG2 = -160.0
NO_WEIGHT = -1e30

V7X_VMEM_LIMIT_BYTES = 56 * 1024 * 1024

BF16 = jnp.bfloat16
F32 = jnp.float32


def _gelu_tanh(x):
    return 0.5 * x * (1.0 + jnp.tanh(GELU_C * (x + 0.044715 * (x * x * x))))


def _silu(x):
    return 0.5 * x * (1.0 + jnp.tanh(0.5 * x))


def _rms_normalize(x_f32, gain_f32):
    ms = jnp.mean(x_f32 * x_f32, axis=-1, keepdims=True)
    return x_f32 * lax.rsqrt(ms + EPS) * gain_f32


def _params(semantics):
    return pltpu.CompilerParams(dimension_semantics=semantics,
                                vmem_limit_bytes=V7X_VMEM_LIMIT_BYTES)


def _gmlp_in_kernel(x_ref, g_ref, wu_ref, wv_ref, wg_ref, ug_ref, v_ref, rinv_ref,
                    xn_ref, ssq_ref, *, width):
    j = pl.program_id(1)

    @pl.when(j == 0)
    def _():
        xn_ref[...] = _rms_normalize(x_ref[...], g_ref[...]).astype(BF16)
        ssq_ref[...] = jnp.zeros_like(ssq_ref)

    xn = xn_ref[...]
    u = jnp.dot(xn, wu_ref[...], preferred_element_type=F32)
    v = jnp.dot(xn, wv_ref[...], preferred_element_type=F32)
    zg = jnp.dot(xn, wg_ref[...], preferred_element_type=F32)
    v = _gelu_tanh(v)
    ug_ref[...] = (_gelu_tanh(u) * _silu(zg)).astype(BF16)
    v_ref[...] = v.astype(BF16)
    ssq_ref[...] += jnp.sum(v * v, axis=-1, keepdims=True)

    @pl.when(j == pl.num_programs(1) - 1)
    def _():
        rinv_ref[...] = lax.rsqrt(ssq_ref[...] * (1.0 / width) + EPS)


def _gmlp_in(x2, gain, w_in_bf16, *, tm, tn):
    tokens, d_model = x2.shape
    width = w_in_bf16.shape[1] // 3
    nj = width // tn
    grid = (tokens // tm, nj)
    w_spec = lambda off: pl.BlockSpec((d_model, tn), lambda i, j, off=off: (0, j + off))
    return pl.pallas_call(
        functools.partial(_gmlp_in_kernel, width=width),
        grid=grid,
        in_specs=[
            pl.BlockSpec((tm, d_model), lambda i, j: (i, 0)),
            pl.BlockSpec((1, d_model), lambda i, j: (0, 0)),
            w_spec(0), w_spec(nj), w_spec(2 * nj),
        ],
        out_specs=[
            pl.BlockSpec((tm, tn), lambda i, j: (i, j)),
            pl.BlockSpec((tm, tn), lambda i, j: (i, j)),
            pl.BlockSpec((tm, 1), lambda i, j: (i, 0)),
        ],
        out_shape=[
            jax.ShapeDtypeStruct((tokens, width), BF16),
            jax.ShapeDtypeStruct((tokens, width), BF16),
            jax.ShapeDtypeStruct((tokens, 1), F32),
        ],
        scratch_shapes=[pltpu.VMEM((tm, d_model), BF16), pltpu.VMEM((tm, 1), F32)],
        compiler_params=_params(("parallel", "arbitrary")),
        name="gmlp_in_proj",
    )(x2, gain, w_in_bf16, w_in_bf16, w_in_bf16)


def _gmlp_out_kernel(x_ref, v_ref, ug_ref, rinv_ref, vg_ref, ws_ref, bs_ref, wo_ref, h_ref, *, tm):
    g = pl.program_id(1)

    @pl.when(g == 0)
    def _():
        h_ref[...] = x_ref[...]

    row = lax.broadcasted_iota(jnp.int32, (CHUNK, CHUNK), 0)
    col = lax.broadcasted_iota(jnp.int32, (CHUNK, CHUNK), 1)
    ws = jnp.where(row >= col, ws_ref[0], 0.0).astype(BF16)
    bias = bs_ref[0]
    vn = (v_ref[...].astype(F32) * rinv_ref[...] * vg_ref[...]).astype(BF16)
    ys = []
    for c in range(tm // CHUNK):
        rows = slice(c * CHUNK, (c + 1) * CHUNK)
        mixed = jnp.dot(ws, vn[rows], preferred_element_type=F32) + bias
        ys.append((ug_ref[rows, :].astype(F32) * mixed).astype(BF16))
    y = jnp.concatenate(ys, axis=0)
    h_ref[...] += jnp.dot(y, wo_ref[...], preferred_element_type=F32)


def _gmlp_out(x2, v, ug, rinv, v_gain, w_s, b_s3, w_out_bf16, *, tm):
    tokens, d_model = x2.shape
    width = v.shape[1]
    gd = width // GMLP_GROUPS
    grid = (tokens // tm, GMLP_GROUPS)
    return pl.pallas_call(
        functools.partial(_gmlp_out_kernel, tm=tm),
        grid=grid,
        in_specs=[
            pl.BlockSpec((tm, d_model), lambda i, g: (i, 0)),
            pl.BlockSpec((tm, gd), lambda i, g: (i, g)),
            pl.BlockSpec((tm, gd), lambda i, g: (i, g)),
            pl.BlockSpec((tm, 1), lambda i, g: (i, 0)),
            pl.BlockSpec((1, gd), lambda i, g: (0, g)),
            pl.BlockSpec((1, CHUNK, CHUNK), lambda i, g: (g, 0, 0)),
            pl.BlockSpec((1, CHUNK, 1), lambda i, g: (g, 0, 0)),
            pl.BlockSpec((gd, d_model), lambda i, g: (g, 0)),
        ],
        out_specs=pl.BlockSpec((tm, d_model), lambda i, g: (i, 0)),
        out_shape=jax.ShapeDtypeStruct((tokens, d_model), F32),
        compiler_params=_params(("parallel", "arbitrary")),
        name="gmlp_mix_out_proj",
    )(x2, v, ug, rinv, v_gain, w_s, b_s3, w_out_bf16)


def _sb_in_kernel(h_ref, g_ref, wq_ref, wk_ref, wv_ref, wg_ref, q_ref, k_ref, v_ref, sg_ref, hn_ref):
    j = pl.program_id(1)

    @pl.when(j == 0)
    def _():
        hn_ref[...] = _rms_normalize(h_ref[...], g_ref[...]).astype(BF16)

    hn = hn_ref[...]
    q = jnp.dot(hn, wq_ref[...], preferred_element_type=F32)
    q_ref[...] = (q * (SB_HEAD_DIM ** -0.5 * LOG2_E)).astype(BF16)
    k_ref[...] = jnp.dot(hn, wk_ref[...], preferred_element_type=F32).astype(BF16)
    v_ref[...] = jnp.dot(hn, wv_ref[...], preferred_element_type=F32).astype(BF16)
    sg_ref[...] = _silu(jnp.dot(hn, wg_ref[...], preferred_element_type=F32)).astype(BF16)


def _sb_in(h, gain, w_in_bf16, *, tm, tn):
    tokens, d_model = h.shape
    width = w_in_bf16.shape[1] // 4
    nj = width // tn
    grid = (tokens // tm, nj)
    w_spec = lambda off: pl.BlockSpec((d_model, tn), lambda i, j, off=off: (0, j + off))
    out_spec = pl.BlockSpec((tm, tn), lambda i, j: (i, j))
    out_shape = jax.ShapeDtypeStruct((tokens, width), BF16)
    return pl.pallas_call(
        _sb_in_kernel,
        grid=grid,
        in_specs=[
            pl.BlockSpec((tm, d_model), lambda i, j: (i, 0)),
            pl.BlockSpec((1, d_model), lambda i, j: (0, 0)),
            w_spec(0), w_spec(nj), w_spec(2 * nj), w_spec(3 * nj),
        ],
        out_specs=[out_spec] * 4,
        out_shape=[out_shape] * 4,
        scratch_shapes=[pltpu.VMEM((tm, d_model), BF16)],
        compiler_params=_params(("parallel", "arbitrary")),
        name="sb_in_proj",
    )(h, gain, w_in_bf16, w_in_bf16, w_in_bf16, w_in_bf16)


def _sb_attn_kernel(q_ref, k_ref, v_ref, sg_ref, o_ref, *, tq, nh):
    qi = pl.program_id(2)
    row = lax.broadcasted_iota(jnp.int32, (tq, tq), 0)
    col = lax.broadcasted_iota(jnp.int32, (tq, tq), 1)
    suffix = (row > col).astype(BF16)
    causal = col < row
    lanes = [slice(h * SB_HEAD_DIM, (h + 1) * SB_HEAD_DIM) for h in range(nh)]
    qs = [q_ref[:, lanes[h]] for h in range(nh)]

    def block(kb, h, carry, masked):
        keys = pl.ds(pl.multiple_of(kb * tq, tq), tq)
        z = lax.dot_general(qs[h], k_ref[keys, lanes[h]], (((1,), (1,)), ((), ())),
                            preferred_element_type=F32)
        log1p_term = jnp.log2(1.0 + jnp.exp2(-jnp.abs(z)))
        log_beta = jnp.minimum(z, 0.0) - log1p_term
        log_fail = log_beta - z
        if masked:
            log_fail = jnp.where(causal, log_fail, 0.0)
        tail = jnp.dot(log_fail.astype(BF16), suffix, preferred_element_type=F32)
        a = jnp.exp2(log_beta + tail + carry)
        if masked:
            a = jnp.where(causal, a, 0.0)
        out = jnp.dot(a.astype(BF16), v_ref[keys, lanes[h]], preferred_element_type=F32)
        return jnp.sum(log_fail, axis=-1, keepdims=True), out

    prev = jnp.maximum(qi - 1, 0)
    no_prev = jnp.where(qi > 0, 0.0, NO_WEIGHT)
    state = []
    for h in range(nh):
        sum_d, out_d = block(qi, h, jnp.zeros((tq, 1), F32), masked=True)
        sum_p, out_p = block(prev, h, sum_d + no_prev, masked=False)
        state.append((sum_d + sum_p, out_d + out_p))

    def live(state):
        top = state[0][0]
        for h in range(1, nh):
            top = jnp.maximum(top, state[h][0])
        return jnp.max(top) >= ZERO_WEIGHT_LOG2

    def cond(loop_state):
        kb, alive, _ = loop_state
        return jnp.logical_and(kb >= 0, alive)

    def body(loop_state):
        kb, _, state = loop_state
        new_state = []
        for h in range(nh):
            carry, acc = state[h]
            block_sum, out = block(kb, h, carry, masked=False)
            new_state.append((carry + block_sum, acc + out))
        new_state = tuple(new_state)
        return kb - 1, live(new_state), new_state

    state = tuple(state)
    _, _, state = lax.while_loop(cond, body, (qi - 2, live(state), state))
    for h in range(nh):
        o_ref[:, lanes[h]] = (state[h][1] * sg_ref[:, lanes[h]].astype(F32)).astype(BF16)


def _sb_attn(q, k, v, sg, *, batch, seq, tq, nh):
    tokens, width = q.shape
    heads = width // SB_HEAD_DIM
    nq = seq // tq
    grid = (batch, heads // nh, nq)
    q_spec = pl.BlockSpec((tq, nh * SB_HEAD_DIM), lambda b, h, i: (b * nq + i, h))
    kv_spec = pl.BlockSpec((seq, nh * SB_HEAD_DIM), lambda b, h, i: (b, h))
    return pl.pallas_call(
        functools.partial(_sb_attn_kernel, tq=tq, nh=nh),
        grid=grid,
        in_specs=[q_spec, kv_spec, kv_spec, q_spec],
        out_specs=q_spec,
        out_shape=jax.ShapeDtypeStruct((tokens, width), BF16),
        compiler_params=_params(("parallel", "parallel", "arbitrary")),
        name="sb_attention",
    )(q, k, v, sg)


def _sb_out_kernel(h_ref, o_ref, wo_ref, g_ref, out_ref):
    h2 = h_ref[...] + jnp.dot(o_ref[...], wo_ref[...], preferred_element_type=F32)
    out_ref[...] = _rms_normalize(h2, g_ref[...])


def _sb_out(h, og, w_out_bf16, final_g, *, tm):
    tokens, d_model = h.shape
    width = og.shape[1]
    return pl.pallas_call(
        _sb_out_kernel,
        grid=(tokens // tm,),
        in_specs=[
            pl.BlockSpec((tm, d_model), lambda i: (i, 0)),
            pl.BlockSpec((tm, width), lambda i: (i, 0)),
            pl.BlockSpec((width, d_model), lambda i: (0, 0)),
            pl.BlockSpec((1, d_model), lambda i: (0, 0)),
        ],
        out_specs=pl.BlockSpec((tm, d_model), lambda i: (i, 0)),
        out_shape=jax.ShapeDtypeStruct((tokens, d_model), F32),
        compiler_params=_params(("parallel",)),
        name="sb_out_proj_norm",
    )(h, og, w_out_bf16, final_g)


def kernel(x, norm_g, a_w_in, a_v_norm_g, a_w_s, a_b_s, a_w_out, b_w_in, b_w_out, final_g):
    batch, seq, d_model = x.shape
    assert norm_g.shape[0] == 2 and a_w_in.shape[0] == 1 and b_w_in.shape[0] == 1
    x2 = x.reshape(batch * seq, d_model)
    ug, v, rinv = _gmlp_in(x2, norm_g[0][None, :], a_w_in[0].astype(BF16), tm=1024, tn=512)
    h1 = _gmlp_out(x2, v, ug, rinv, a_v_norm_g[0][None, :], a_w_s[0], a_b_s[0][:, :, None],
                   a_w_out[0].astype(BF16), tm=1024)
    q, k, v, sg = _sb_in(h1, norm_g[1][None, :], b_w_in[0].astype(BF16), tm=1024, tn=512)
    og = _sb_attn(q, k, v, sg, batch=batch, seq=seq, tq=256, nh=2)
    out = _sb_out(h1, og, b_w_out[0].astype(BF16), final_g[None, :], tm=512)
    return out.reshape(batch, seq, d_model)
```

```python
import functools

import jax
import jax.numpy as jnp
from jax import lax
from jax.experimental import pallas as pl
from jax.experimental.pallas import tpu as pltpu

EPS = 1e-6
CHUNK = 128
GMLP_GROUPS = 16
SB_HEAD_DIM = 128
GELU_C = 0.7978845608028654
LOG2_E = 1.4426950408889634
ZERO_WEIGHT_LOG2 = -160.0
NO_WEIGHT = -1e30

V7X_VMEM_LIMIT_BYTES = 56 * 1024 * 1024

BF16 = jnp.bfloat16
F32 = jnp.float32


def _gelu_tanh(x):
    return 0.5 * x * (1.0 + jnp.tanh(GELU_C * (x + 0.044715 * (x * x * x))))


def _silu(x):
    return 0.5 * x * (1.0 + jnp.tanh(0.5 * x))


def _rms_normalize(x_f32, gain_f32):
    ms = jnp.mean(x_f32 * x_f32, axis=-1, keepdims=True)
    return x_f32 * lax.rsqrt(ms + EPS) * gain_f32


def _params(semantics):
    return pltpu.CompilerParams(dimension_semantics=semantics,
                                vmem_limit_bytes=V7X_VMEM_LIMIT_BYTES)


def _gmlp_in_kernel(x_ref, g_ref, wu_ref, wv_ref, wg_ref, ug_ref, v_ref, rinv_ref,
                    xn_ref, ssq_ref, *, width):
    j = pl.program_id(1)

    @pl.when(j == 0)
    def _():
        xn_ref[...] = _rms_normalize(x_ref[...], g_ref[...]).astype(BF16)
        ssq_ref[...] = jnp.zeros_like(ssq_ref)

    xn = xn_ref[...]
    u = jnp.dot(xn, wu_ref[...], preferred_element_type=F32)
    v = jnp.dot(xn, wv_ref[...], preferred_element_type=F32)
    zg = jnp.dot(xn, wg_ref[...], preferred_element_type=F32)
    v = _gelu_tanh(v)
    ug_ref[...] = (_gelu_tanh(u) * _silu(zg)).astype(BF16)
    v_ref[...] = v.astype(BF16)
    ssq_ref[...] += jnp.sum(v * v, axis=-1, keepdims=True)

    @pl.when(j == pl.num_programs(1) - 1)
    def _():
        rinv_ref[...] = lax.rsqrt(ssq_ref[...] * (1.0 / width) + EPS)


def _gmlp_in(x2, gain, w_in_bf16, *, tm, tn):
    tokens, d_model = x2.shape
    width = w_in_bf16.shape[1] // 3
    nj = width // tn
    grid = (tokens // tm, nj)
    w_spec = lambda off: pl.BlockSpec((d_model, tn), lambda i, j, off=off: (0, j + off))
    return pl.pallas_call(
        functools.partial(_gmlp_in_kernel, width=width),
        grid=grid,
        in_specs=[
            pl.BlockSpec((tm, d_model), lambda i, j: (i, 0)),
            pl.BlockSpec((1, d_model), lambda i, j: (0, 0)),
            w_spec(0), w_spec(nj), w_spec(2 * nj),
        ],
        out_specs=[
            pl.BlockSpec((tm, tn), lambda i, j: (i, j)),
            pl.BlockSpec((tm, tn), lambda i, j: (i, j)),
            pl.BlockSpec((tm, 1), lambda i, j: (i, 0)),
        ],
        out_shape=[
            jax.ShapeDtypeStruct((tokens, width), BF16),
            jax.ShapeDtypeStruct((tokens, width), BF16),
            jax.ShapeDtypeStruct((tokens, 1), F32),
        ],
        scratch_shapes=[pltpu.VMEM((tm, d_model), BF16), pltpu.VMEM((tm, 1), F32)],
        compiler_params=_params(("parallel", "arbitrary")),
        name="gmlp_in_proj",
    )(x2, gain, w_in_bf16, w_in_bf16, w_in_bf16)


def _gmlp_out_kernel(x_ref, v_ref, ug_ref, rinv_ref, vg_ref, ws_ref, bs_ref, wo_ref, h_ref, *, tm, gk, gd):
    step = pl.program_id(1)

    @pl.when(step == 0)
    def _():
        h_ref[...] = x_ref[...]

    row = lax.broadcasted_iota(jnp.int32, (CHUNK, CHUNK), 0)
    col = lax.broadcasted_iota(jnp.int32, (CHUNK, CHUNK), 1)
    rinv = rinv_ref[...]
    ys = []
    for g in range(gk):
        cols = slice(g * gd, (g + 1) * gd)
        ws = jnp.where(row >= col, ws_ref[g], 0.0).astype(BF16)
        bias = bs_ref[g]
        vn = (v_ref[:, cols].astype(F32) * rinv * vg_ref[:, cols]).astype(BF16)
        yg = []
        for c in range(tm // CHUNK):
            rows = slice(c * CHUNK, (c + 1) * CHUNK)
            mixed = jnp.dot(ws, vn[rows], preferred_element_type=F32) + bias
            yg.append((ug_ref[rows, cols].astype(F32) * mixed).astype(BF16))
        ys.append(jnp.concatenate(yg, axis=0))
    y = jnp.concatenate(ys, axis=1)
    h_ref[...] += jnp.dot(y, wo_ref[...], preferred_element_type=F32)


def _gmlp_out(x2, v, ug, rinv, v_gain, w_s, b_s3, w_out_bf16, *, tm, gk):
    tokens, d_model = x2.shape
    width = v.shape[1]
    gd = width // GMLP_GROUPS
    grid = (tokens // tm, GMLP_GROUPS // gk)
    return pl.pallas_call(
        functools.partial(_gmlp_out_kernel, tm=tm, gk=gk, gd=gd),
        grid=grid,
        in_specs=[
            pl.BlockSpec((tm, d_model), lambda i, g: (i, 0)),
            pl.BlockSpec((tm, gk * gd), lambda i, g: (i, g)),
            pl.BlockSpec((tm, gk * gd), lambda i, g: (i, g)),
            pl.BlockSpec((tm, 1), lambda i, g: (i, 0)),
            pl.BlockSpec((1, gk * gd), lambda i, g: (0, g)),
            pl.BlockSpec((gk, CHUNK, CHUNK), lambda i, g: (g, 0, 0)),
            pl.BlockSpec((gk, CHUNK, 1), lambda i, g: (g, 0, 0)),
            pl.BlockSpec((gk * gd, d_model), lambda i, g: (g, 0)),
        ],
        out_specs=pl.BlockSpec((tm, d_model), lambda i, g: (i, 0)),
        out_shape=jax.ShapeDtypeStruct((tokens, d_model), F32),
        compiler_params=_params(("parallel", "arbitrary")),
        name="gmlp_mix_out_proj",
    )(x2, v, ug, rinv, v_gain, w_s, b_s3, w_out_bf16)


def _sb_in_kernel(h_ref, g_ref, wq_ref, wk_ref, wv_ref, wg_ref, q_ref, k_ref, v_ref, sg_ref, hn_ref):
    j = pl.program_id(1)

    @pl.when(j == 0)
    def _():
        hn_ref[...] = _rms_normalize(h_ref[...], g_ref[...]).astype(BF16)

    hn = hn_ref[...]
    q = jnp.dot(hn, wq_ref[...], preferred_element_type=F32)
    q_ref[...] = (q * (SB_HEAD_DIM ** -0.5 * LOG2_E)).astype(BF16)
    k_ref[...] = jnp.dot(hn, wk_ref[...], preferred_element_type=F32).astype(BF16)
    v_ref[...] = jnp.dot(hn, wv_ref[...], preferred_element_type=F32).astype(BF16)
    sg_ref[...] = _silu(jnp.dot(hn, wg_ref[...], preferred_element_type=F32)).astype(BF16)


def _sb_in(h, gain, w_in_bf16, *, tm, tn):
    tokens, d_model = h.shape
    width = w_in_bf16.shape[1] // 4
    nj = width // tn
    grid = (tokens // tm, nj)
    w_spec = lambda off: pl.BlockSpec((d_model, tn), lambda i, j, off=off: (0, j + off))
    out_spec = pl.BlockSpec((tm, tn), lambda i, j: (i, j))
    out_shape = jax.ShapeDtypeStruct((tokens, width), BF16)
    return pl.pallas_call(
        _sb_in_kernel,
        grid=grid,
        in_specs=[
            pl.BlockSpec((tm, d_model), lambda i, j: (i, 0)),
            pl.BlockSpec((1, d_model), lambda i, j: (0, 0)),
            w_spec(0), w_spec(nj), w_spec(2 * nj), w_spec(3 * nj),
        ],
        out_specs=[out_spec] * 4,
        out_shape=[out_shape] * 4,
        scratch_shapes=[pltpu.VMEM((tm, d_model), BF16)],
        compiler_params=_params(("parallel", "arbitrary")),
        name="sb_in_proj",
    )(h, gain, w_in_bf16, w_in_bf16, w_in_bf16, w_in_bf16)


def _sb_attn_kernel(q_ref, k_ref, v_ref, sg_ref, o_ref, *, tq, nh):
    qi = pl.program_id(2)
    row = lax.broadcasted_iota(jnp.int32, (tq, tq), 0)
    col = lax.broadcasted_iota(jnp.int32, (tq, tq), 1)
    suffix = (row > col).astype(BF16)
    causal = col < row
    lanes = [slice(h * SB_HEAD_DIM, (h + 1) * SB_HEAD_DIM) for h in range(nh)]
    qs = [q_ref[:, lanes[h]] for h in range(nh)]

    def block(kb, h, carry, masked):
        keys = pl.ds(pl.multiple_of(kb * tq, tq), tq)
        z = lax.dot_general(qs[h], k_ref[keys, lanes[h]], (((1,), (1,)), ((), ())),
                            preferred_element_type=F32)
        log1p_term = jnp.log2(1.0 + jnp.exp2(-jnp.abs(z)))
        log_beta = jnp.minimum(z, 0.0) - log1p_term
        log_fail = log_beta - z
        if masked:
            log_fail = jnp.where(causal, log_fail, 0.0)
        tail = jnp.dot(log_fail.astype(BF16), suffix, preferred_element_type=F32)
        a = jnp.exp2(log_beta + tail + carry)
        if masked:
            a = jnp.where(causal, a, 0.0)
        out = jnp.dot(a.astype(BF16), v_ref[keys, lanes[h]], preferred_element_type=F32)
        return jnp.sum(log_fail, axis=-1, keepdims=True), out

    prev = jnp.maximum(qi - 1, 0)
    no_prev = jnp.where(qi > 0, 0.0, NO_WEIGHT)
    state = []
    for h in range(nh):
        sum_d, out_d = block(qi, h, jnp.zeros((tq, 1), F32), masked=True)
        sum_p, out_p = block(prev, h, sum_d + no_prev, masked=False)
        state.append((sum_d + sum_p, out_d + out_p))

    def live(state):
        top = state[0][0]
        for h in range(1, nh):
            top = jnp.maximum(top, state[h][0])
        return jnp.max(top) >= ZERO_WEIGHT_LOG2

    def cond(loop_state):
        kb, alive, _ = loop_state
        return jnp.logical_and(kb >= 0, alive)

    def body(loop_state):
        kb, _, state = loop_state
        new_state = []
        for h in range(nh):
            carry, acc = state[h]
            block_sum, out = block(kb, h, carry, masked=False)
            new_state.append((carry + block_sum, acc + out))
        new_state = tuple(new_state)
        return kb - 1, live(new_state), new_state

    state = tuple(state)
    _, _, state = lax.while_loop(cond, body, (qi - 2, live(state), state))
    for h in range(nh):
        o_ref[:, lanes[h]] = (state[h][1] * sg_ref[:, lanes[h]].astype(F32)).astype(BF16)


def _sb_attn(q, k, v, sg, *, batch, seq, tq, nh):
    tokens, width = q.shape
    heads = width // SB_HEAD_DIM
    nq = seq // tq
    grid = (batch, heads // nh, nq)
    q_spec = pl.BlockSpec((tq, nh * SB_HEAD_DIM), lambda b, h, i: (b * nq + i, h))
    kv_spec = pl.BlockSpec((seq, nh * SB_HEAD_DIM), lambda b, h, i: (b, h))
    return pl.pallas_call(
        functools.partial(_sb_attn_kernel, tq=tq, nh=nh),
        grid=grid,
        in_specs=[q_spec, kv_spec, kv_spec, q_spec],
        out_specs=q_spec,
        out_shape=jax.ShapeDtypeStruct((tokens, width), BF16),
        compiler_params=_params(("parallel", "parallel", "arbitrary")),
        name="sb_attention",
    )(q, k, v, sg)


def _sb_out_kernel(h_ref, o_ref, wo_ref, g_ref, out_ref):
    h2 = h_ref[...] + jnp.dot(o_ref[...], wo_ref[...], preferred_element_type=F32)
    out_ref[...] = _rms_normalize(h2, g_ref[...])


def _sb_out(h, og, w_out_bf16, final_g, *, tm):
    tokens, d_model = h.shape
    width = og.shape[1]
    return pl.pallas_call(
        _sb_out_kernel,
        grid=(tokens // tm,),
        in_specs=[
            pl.BlockSpec((tm, d_model), lambda i: (i, 0)),
            pl.BlockSpec((tm, width), lambda i: (i, 0)),
            pl.BlockSpec((width, d_model), lambda i: (0, 0)),
            pl.BlockSpec((1, d_model), lambda i: (0, 0)),
        ],
        out_specs=pl.BlockSpec((tm, d_model), lambda i: (i, 0)),
        out_shape=jax.ShapeDtypeStruct((tokens, d_model), F32),
        compiler_params=_params(("parallel",)),
        name="sb_out_proj_norm",
    )(h, og, w_out_bf16, final_g)


def kernel(x, norm_g, a_w_in, a_v_norm_g, a_w_s, a_b_s, a_w_out, b_w_in, b_w_out, final_g):
    batch, seq, d_model = x.shape
    assert norm_g.shape[0] == 2 and a_w_in.shape[0] == 1 and b_w_in.shape[0] == 1
    x2 = x.reshape(batch * seq, d_model)
    ug, v, rinv = _gmlp_in(x2, norm_g[0][None, :], a_w_in[0].astype(BF16), tm=1024, tn=512)
    h1 = _gmlp_out(x2, v, ug, rinv, a_v_norm_g[0][None, :], a_w_s[0], a_b_s[0][:, :, None],
                   a_w_out[0].astype(BF16), tm=1024, gk=4)
    q, k, v, sg = _sb_in(h1, norm_g[1][None, :], b_w_in[0].astype(BF16), tm=1024, tn=512)
    og = _sb_attn(q, k, v, sg, batch=batch, seq=seq, tq=256, nh=4)
    out = _sb_out(h1, og, b_w_out[0].astype(BF16), final_g[None, :], tm=512)
    return out.reshape(batch, seq, d_model)
```

```python
import functools

import jax
import jax.numpy as jnp
from jax import lax
from jax.experimental import pallas as pl
from jax.experimental.pallas import tpu as pltpu

EPS = 1e-6
CHUNK = 128
GMLP_GROUPS = 16
SB_HEAD_DIM = 128
GELU_C = 0.7978845608028654
LOG2_E = 1.4426950408889634
ZERO_WEIGHT_LOG2 = -160.0
NO_WEIGHT = -1e30

V7X_VMEM_LIMIT_BYTES = 56 * 1024 * 1024

BF16 = jnp.bfloat16
F32 = jnp.float32


def _gelu_tanh(x):
    return 0.5 * x * (1.0 + jnp.tanh(GELU_C * (x + 0.044715 * (x * x * x))))


def _silu(x):
    return 0.5 * x * (1.0 + jnp.tanh(0.5 * x))


def _rms_normalize(x_f32, gain_f32):
    ms = jnp.mean(x_f32 * x_f32, axis=-1, keepdims=True)
    return x_f32 * lax.rsqrt(ms + EPS) * gain_f32


def _dot_w(x_bf16, w_ref):
    return jnp.dot(x_bf16, w_ref[...].astype(BF16), preferred_element_type=F32)


def _params(semantics):
    return pltpu.CompilerParams(dimension_semantics=semantics,
                                vmem_limit_bytes=V7X_VMEM_LIMIT_BYTES)


def _gmlp_in_kernel(x_ref, g_ref, wu_ref, wv_ref, wg_ref, ug_ref, v_ref, rinv_ref,
                    xn_ref, ssq_ref, *, width):
    j = pl.program_id(1)

    @pl.when(j == 0)
    def _():
        xn_ref[...] = _rms_normalize(x_ref[...], g_ref[...]).astype(BF16)
        ssq_ref[...] = jnp.zeros_like(ssq_ref)

    xn = xn_ref[...]
    u = _dot_w(xn, wu_ref)
    v = _dot_w(xn, wv_ref)
    zg = _dot_w(xn, wg_ref)
    v = _gelu_tanh(v)
    ug_ref[...] = (_gelu_tanh(u) * _silu(zg)).astype(BF16)
    v_ref[...] = v.astype(BF16)
    ssq_ref[...] += jnp.sum(v * v, axis=-1, keepdims=True)

    @pl.when(j == pl.num_programs(1) - 1)
    def _():
        rinv_ref[...] = lax.rsqrt(ssq_ref[...] * (1.0 / width) + EPS)


def _gmlp_in(x2, gain, w_in, *, tm, tn):
    tokens, d_model = x2.shape
    width = w_in.shape[1] // 3
    nj = width // tn
    grid = (tokens // tm, nj)
    w_spec = lambda off: pl.BlockSpec((d_model, tn), lambda i, j, off=off: (0, j + off))
    return pl.pallas_call(
        functools.partial(_gmlp_in_kernel, width=width),
        grid=grid,
        in_specs=[
            pl.BlockSpec((tm, d_model), lambda i, j: (i, 0)),
            pl.BlockSpec((1, d_model), lambda i, j: (0, 0)),
            w_spec(0), w_spec(nj), w_spec(2 * nj),
        ],
        out_specs=[
            pl.BlockSpec((tm, tn), lambda i, j: (i, j)),
            pl.BlockSpec((tm, tn), lambda i, j: (i, j)),
            pl.BlockSpec((tm, 1), lambda i, j: (i, 0)),
        ],
        out_shape=[
            jax.ShapeDtypeStruct((tokens, width), BF16),
            jax.ShapeDtypeStruct((tokens, width), BF16),
            jax.ShapeDtypeStruct((tokens, 1), F32),
        ],
        scratch_shapes=[pltpu.VMEM((tm, d_model), BF16), pltpu.VMEM((tm, 1), F32)],
        compiler_params=_params(("parallel", "arbitrary")),
        name="gmlp_in_proj",
    )(x2, gain, w_in, w_in, w_in)


def _gmlp_out_kernel(x_ref, v_ref, ug_ref, rinv_ref, vg_ref, ws_ref, bs_ref, wo_ref, h_ref, *, tm, gk, gd):
    step = pl.program_id(1)

    @pl.when(step == 0)
    def _():
        h_ref[...] = x_ref[...]

    row = lax.broadcasted_iota(jnp.int32, (CHUNK, CHUNK), 0)
    col = lax.broadcasted_iota(jnp.int32, (CHUNK, CHUNK), 1)
    rinv = rinv_ref[...]
    ys = []
    for g in range(gk):
        cols = slice(g * gd, (g + 1) * gd)
        ws = jnp.where(row >= col, ws_ref[g], 0.0).astype(BF16)
        bias = bs_ref[g]
        vn = (v_ref[:, cols].astype(F32) * rinv * vg_ref[:, cols]).astype(BF16)
        yg = []
        for c in range(tm // CHUNK):
            rows = slice(c * CHUNK, (c + 1) * CHUNK)
            mixed = jnp.dot(ws, vn[rows], preferred_element_type=F32) + bias
            yg.append((ug_ref[rows, cols].astype(F32) * mixed).astype(BF16))
        ys.append(jnp.concatenate(yg, axis=0))
    y = jnp.concatenate(ys, axis=1)
    h_ref[...] += jnp.dot(y, wo_ref[...], preferred_element_type=F32)


def _gmlp_out(x2, v, ug, rinv, v_gain, w_s, b_s3, w_out_bf16, *, tm, gk):
    tokens, d_model = x2.shape
    width = v.shape[1]
    gd = width // GMLP_GROUPS
    grid = (tokens // tm, GMLP_GROUPS // gk)
    return pl.pallas_call(
        functools.partial(_gmlp_out_kernel, tm=tm, gk=gk, gd=gd),
        grid=grid,
        in_specs=[
            pl.BlockSpec((tm, d_model), lambda i, g: (i, 0)),
            pl.BlockSpec((tm, gk * gd), lambda i, g: (i, g)),
            pl.BlockSpec((tm, gk * gd), lambda i, g: (i, g)),
            pl.BlockSpec((tm, 1), lambda i, g: (i, 0)),
            pl.BlockSpec((1, gk * gd), lambda i, g: (0, g)),
            pl.BlockSpec((gk, CHUNK, CHUNK), lambda i, g: (g, 0, 0)),
            pl.BlockSpec((gk, CHUNK, 1), lambda i, g: (g, 0, 0)),
            pl.BlockSpec((gk * gd, d_model), lambda i, g: (g, 0)),
        ],
        out_specs=pl.BlockSpec((tm, d_model), lambda i, g: (i, 0)),
        out_shape=jax.ShapeDtypeStruct((tokens, d_model), F32),
        compiler_params=_params(("parallel", "arbitrary")),
        name="gmlp_mix_out_proj",
    )(x2, v, ug, rinv, v_gain, w_s, b_s3, w_out_bf16)


def _sb_in_kernel(h_ref, g_ref, wq_ref, wk_ref, wv_ref, wg_ref, q_ref, k_ref, v_ref, sg_ref, hn_ref):
    j = pl.program_id(1)

    @pl.when(j == 0)
    def _():
        hn_ref[...] = _rms_normalize(h_ref[...], g_ref[...]).astype(BF16)

    hn = hn_ref[...]
    q = _dot_w(hn, wq_ref)
    q_ref[...] = (q * (SB_HEAD_DIM ** -0.5 * LOG2_E)).astype(BF16)
    k_ref[...] = _dot_w(hn, wk_ref).astype(BF16)
    v_ref[...] = _dot_w(hn, wv_ref).astype(BF16)
    sg_ref[...] = _silu(_dot_w(hn, wg_ref)).astype(BF16)


def _sb_in(h, gain, w_in, *, tm, tn):
    tokens, d_model = h.shape
    width = w_in.shape[1] // 4
    nj = width // tn
    grid = (tokens // tm, nj)
    w_spec = lambda off: pl.BlockSpec((d_model, tn), lambda i, j, off=off: (0, j + off))
    out_spec = pl.BlockSpec((tm, tn), lambda i, j: (i, j))
    out_shape = jax.ShapeDtypeStruct((tokens, width), BF16)
    return pl.pallas_call(
        _sb_in_kernel,
        grid=grid,
        in_specs=[
            pl.BlockSpec((tm, d_model), lambda i, j: (i, 0)),
            pl.BlockSpec((1, d_model), lambda i, j: (0, 0)),
            w_spec(0), w_spec(nj), w_spec(2 * nj), w_spec(3 * nj),
        ],
        out_specs=[out_spec] * 4,
        out_shape=[out_shape] * 4,
        scratch_shapes=[pltpu.VMEM((tm, d_model), BF16)],
        compiler_params=_params(("parallel", "arbitrary")),
        name="sb_in_proj",
    )(h, gain, w_in, w_in, w_in, w_in)


def _sb_attn_kernel(q_ref, k_ref, v_ref, sg_ref, o_ref, *, tq, nh):
    qi = pl.program_id(2)
    row = lax.broadcasted_iota(jnp.int32, (tq, tq), 0)
    col = lax.broadcasted_iota(jnp.int32, (tq, tq), 1)
    suffix = (row > col).astype(BF16)
    causal = col < row
    lanes = [slice(h * SB_HEAD_DIM, (h + 1) * SB_HEAD_DIM) for h in range(nh)]
    qs = [q_ref[:, lanes[h]] for h in range(nh)]

    def block(kb, h, carry, masked):
        keys = pl.ds(pl.multiple_of(kb * tq, tq), tq)
        z = lax.dot_general(qs[h], k_ref[keys, lanes[h]], (((1,), (1,)), ((), ())),
                            preferred_element_type=F32)
        log1p_term = jnp.log2(1.0 + jnp.exp2(-jnp.abs(z)))
        log_beta = jnp.minimum(z, 0.0) - log1p_term
        log_fail = log_beta - z
        if masked:
            log_fail = jnp.where(causal, log_fail, 0.0)
        tail = jnp.dot(log_fail.astype(BF16), suffix, preferred_element_type=F32)
        a = jnp.exp2(log_beta + tail + carry)
        if masked:
            a = jnp.where(causal, a, 0.0)
        out = jnp.dot(a.astype(BF16), v_ref[keys, lanes[h]], preferred_element_type=F32)
        return jnp.sum(log_fail, axis=-1, keepdims=True), out

    prev = jnp.maximum(qi - 1, 0)
    no_prev = jnp.where(qi > 0, 0.0, NO_WEIGHT)
    state = []
    for h in range(nh):
        sum_d, out_d = block(qi, h, jnp.zeros((tq, 1), F32), masked=True)
        sum_p, out_p = block(prev, h, sum_d + no_prev, masked=False)
        state.append((sum_d + sum_p, out_d + out_p))

    def live(state):
        top = state[0][0]
        for h in range(1, nh):
            top = jnp.maximum(top, state[h][0])
        return jnp.max(top) >= ZERO_WEIGHT_LOG2

    def cond(loop_state):
        kb, alive, _ = loop_state
        return jnp.logical_and(kb >= 0, alive)

    def body(loop_state):
        kb, _, state = loop_state
        new_state = []
        for h in range(nh):
            carry, acc = state[h]
            block_sum, out = block(kb, h, carry, masked=False)
            new_state.append((carry + block_sum, acc + out))
        new_state = tuple(new_state)
        return kb - 1, live(new_state), new_state

    state = tuple(state)
    _, _, state = lax.while_loop(cond, body, (qi - 2, live(state), state))
    for h in range(nh):
        o_ref[:, lanes[h]] = (state[h][1] * sg_ref[:, lanes[h]].astype(F32)).astype(BF16)


def _sb_attn(q, k, v, sg, *, batch, seq, tq, nh):
    tokens, width = q.shape
    heads = width // SB_HEAD_DIM
    nq = seq // tq
    grid = (batch, heads // nh, nq)
    q_spec = pl.BlockSpec((tq, nh * SB_HEAD_DIM), lambda b, h, i: (b * nq + i, h))
    kv_spec = pl.BlockSpec((seq, nh * SB_HEAD_DIM), lambda b, h, i: (b, h))
    return pl.pallas_call(
        functools.partial(_sb_attn_kernel, tq=tq, nh=nh),
        grid=grid,
        in_specs=[q_spec, kv_spec, kv_spec, q_spec],
        out_specs=q_spec,
        out_shape=jax.ShapeDtypeStruct((tokens, width), BF16),
        compiler_params=_params(("parallel", "parallel", "arbitrary")),
        name="sb_attention",
    )(q, k, v, sg)


def _sb_out_kernel(h_ref, o_ref, wo_ref, g_ref, out_ref, wo_bf16_ref):
    @pl.when(pl.program_id(0) == 0)
    def _():
        wo_bf16_ref[...] = wo_ref[...].astype(BF16)

    h2 = h_ref[...] + jnp.dot(o_ref[...], wo_bf16_ref[...], preferred_element_type=F32)
    out_ref[...] = _rms_normalize(h2, g_ref[...])


def _sb_out(h, og, w_out, final_g, *, tm):
    tokens, d_model = h.shape
    width = og.shape[1]
    return pl.pallas_call(
        _sb_out_kernel,
        grid=(tokens // tm,),
        in_specs=[
            pl.BlockSpec((tm, d_model), lambda i: (i, 0)),
            pl.BlockSpec((tm, width), lambda i: (i, 0)),
            pl.BlockSpec((width, d_model), lambda i: (0, 0), pipeline_mode=pl.Buffered(1)),
            pl.BlockSpec((1, d_model), lambda i: (0, 0)),
        ],
        out_specs=pl.BlockSpec((tm, d_model), lambda i: (i, 0)),
        out_shape=jax.ShapeDtypeStruct((tokens, d_model), F32),
        scratch_shapes=[pltpu.VMEM((width, d_model), BF16)],
        compiler_params=_params(("arbitrary",)),
        name="sb_out_proj_norm",
    )(h, og, w_out, final_g)


def kernel(x, norm_g, a_w_in, a_v_norm_g, a_w_s, a_b_s, a_w_out, b_w_in, b_w_out, final_g):
    batch, seq, d_model = x.shape
    assert norm_g.shape[0] == 2 and a_w_in.shape[0] == 1 and b_w_in.shape[0] == 1
    x2 = x.reshape(batch * seq, d_model)
    ug, v, rinv = _gmlp_in(x2, norm_g[0][None, :], a_w_in[0], tm=1024, tn=512)
    h1 = _gmlp_out(x2, v, ug, rinv, a_v_norm_g[0][None, :], a_w_s[0], a_b_s[0][:, :, None],
                   a_w_out[0].astype(BF16), tm=1024, gk=4)
    q, k, v, sg = _sb_in(h1, norm_g[1][None, :], b_w_in[0], tm=1024, tn=256)
    og = _sb_attn(q, k, v, sg, batch=batch, seq=seq, tq=256, nh=4)
    out = _sb_out(h1, og, b_w_out[0], final_g[None, :], tm=512)
    return out.reshape(batch, seq, d_model)
```

```python
import functools

import jax
import jax.numpy as jnp
from jax import lax
from jax.experimental import pallas as pl
from jax.experimental.pallas import tpu as pltpu

EPS = 1e-6
CHUNK = 128
GMLP_GROUPS = 16
SB_HEAD_DIM = 128
GELU_C = 0.7978845608028654
LOG2_E = 1.4426950408889634
ZERO_WEIGHT_LOG2 = -160.0
NO_WEIGHT = -1e30

V7X_VMEM_LIMIT_BYTES = 56 * 1024 * 1024

BF16 = jnp.bfloat16
F32 = jnp.float32


def _gelu_tanh(x):
    return 0.5 * x * (1.0 + jnp.tanh(GELU_C * (x + 0.044715 * (x * x * x))))


def _silu(x):
    return 0.5 * x * (1.0 + jnp.tanh(0.5 * x))


def _rms_normalize(x_f32, gain_f32):
    ms = jnp.mean(x_f32 * x_f32, axis=-1, keepdims=True)
    return x_f32 * lax.rsqrt(ms + EPS) * gain_f32


def _dot_w(x_bf16, w_ref):
    return jnp.dot(x_bf16, w_ref[...].astype(BF16), preferred_element_type=F32)


def _params(semantics):
    return pltpu.CompilerParams(dimension_semantics=semantics,
                                vmem_limit_bytes=V7X_VMEM_LIMIT_BYTES)


def _gmlp_in_kernel(x_ref, g_ref, wu_ref, wv_ref, wg_ref, ug_ref, v_ref, rinv_ref,
                    xn_ref, ssq_ref, *, width):
    j = pl.program_id(1)

    @pl.when(j == 0)
    def _():
        xn_ref[...] = _rms_normalize(x_ref[...], g_ref[...]).astype(BF16)
        ssq_ref[...] = jnp.zeros_like(ssq_ref)

    xn = xn_ref[...]
    u = _dot_w(xn, wu_ref)
    v = _dot_w(xn, wv_ref)
    zg = _dot_w(xn, wg_ref)
    v = _gelu_tanh(v)
    ug_ref[...] = (_gelu_tanh(u) * _silu(zg)).astype(BF16)
    v_ref[...] = v.astype(BF16)
    ssq_ref[...] += jnp.sum(v * v, axis=-1, keepdims=True)

    @pl.when(j == pl.num_programs(1) - 1)
    def _():
        rinv_ref[...] = lax.rsqrt(ssq_ref[...] * (1.0 / width) + EPS)


def _gmlp_in(x2, gain, w_in, *, tm, tn):
    tokens, d_model = x2.shape
    width = w_in.shape[1] // 3
    nj = width // tn
    grid = (tokens // tm, nj)
    w_spec = lambda off: pl.BlockSpec((d_model, tn), lambda i, j, off=off: (0, j + off))
    return pl.pallas_call(
        functools.partial(_gmlp_in_kernel, width=width),
        grid=grid,
        in_specs=[
            pl.BlockSpec((tm, d_model), lambda i, j: (i, 0)),
            pl.BlockSpec((1, d_model), lambda i, j: (0, 0)),
            w_spec(0), w_spec(nj), w_spec(2 * nj),
        ],
        out_specs=[
            pl.BlockSpec((tm, tn), lambda i, j: (i, j)),
            pl.BlockSpec((tm, tn), lambda i, j: (i, j)),
            pl.BlockSpec((tm, 1), lambda i, j: (i, 0)),
        ],
        out_shape=[
            jax.ShapeDtypeStruct((tokens, width), BF16),
            jax.ShapeDtypeStruct((tokens, width), BF16),
            jax.ShapeDtypeStruct((tokens, 1), F32),
        ],
        scratch_shapes=[pltpu.VMEM((tm, d_model), BF16), pltpu.VMEM((tm, 1), F32)],
        compiler_params=_params(("parallel", "arbitrary")),
        name="gmlp_in_proj",
    )(x2, gain, w_in, w_in, w_in)


def _gmlp_out_kernel(x_ref, v_ref, ug_ref, rinv_ref, vg_ref, ws_ref, bs_ref, wo_ref, ng_ref,
                     h_ref, hn_ref, *, tm, gk, gd):
    step = pl.program_id(1)

    @pl.when(step == 0)
    def _():
        h_ref[...] = x_ref[...]

    row = lax.broadcasted_iota(jnp.int32, (CHUNK, CHUNK), 0)
    col = lax.broadcasted_iota(jnp.int32, (CHUNK, CHUNK), 1)
    rinv = rinv_ref[...]
    ys = []
    for g in range(gk):
        cols = slice(g * gd, (g + 1) * gd)
        ws = jnp.where(row >= col, ws_ref[g], 0.0).astype(BF16)
        bias = bs_ref[g]
        vn = (v_ref[:, cols].astype(F32) * rinv * vg_ref[:, cols]).astype(BF16)
        yg = []
        for c in range(tm // CHUNK):
            rows = slice(c * CHUNK, (c + 1) * CHUNK)
            mixed = jnp.dot(ws, vn[rows], preferred_element_type=F32) + bias
            yg.append((ug_ref[rows, cols].astype(F32) * mixed).astype(BF16))
        ys.append(jnp.concatenate(yg, axis=0))
    y = jnp.concatenate(ys, axis=1)
    h_ref[...] += jnp.dot(y, wo_ref[...], preferred_element_type=F32)

    @pl.when(step == pl.num_programs(1) - 1)
    def _():
        hn_ref[...] = _rms_normalize(h_ref[...], ng_ref[...]).astype(BF16)


def _gmlp_out(x2, v, ug, rinv, v_gain, w_s, b_s3, w_out_bf16, next_gain, *, tm, gk):
    tokens, d_model = x2.shape
    width = v.shape[1]
    gd = width // GMLP_GROUPS
    grid = (tokens // tm, GMLP_GROUPS // gk)
    return pl.pallas_call(
        functools.partial(_gmlp_out_kernel, tm=tm, gk=gk, gd=gd),
        grid=grid,
        in_specs=[
            pl.BlockSpec((tm, d_model), lambda i, g: (i, 0)),
            pl.BlockSpec((tm, gk * gd), lambda i, g: (i, g)),
            pl.BlockSpec((tm, gk * gd), lambda i, g: (i, g)),
            pl.BlockSpec((tm, 1), lambda i, g: (i, 0)),
            pl.BlockSpec((1, gk * gd), lambda i, g: (0, g)),
            pl.BlockSpec((gk, CHUNK, CHUNK), lambda i, g: (g, 0, 0)),
            pl.BlockSpec((gk, CHUNK, 1), lambda i, g: (g, 0, 0)),
            pl.BlockSpec((gk * gd, d_model), lambda i, g: (g, 0)),
            pl.BlockSpec((1, d_model), lambda i, g: (0, 0)),
        ],
        out_specs=[pl.BlockSpec((tm, d_model), lambda i, g: (i, 0))] * 2,
        out_shape=[jax.ShapeDtypeStruct((tokens, d_model), F32),
                   jax.ShapeDtypeStruct((tokens, d_model), BF16)],
        compiler_params=_params(("parallel", "arbitrary")),
        name="gmlp_mix_out_proj",
    )(x2, v, ug, rinv, v_gain, w_s, b_s3, w_out_bf16, next_gain)


def _sb_in_kernel(hn_ref, wq_ref, wk_ref, wv_ref, wg_ref, q_ref, k_ref, v_ref, sg_ref, w_bf16_ref):
    @pl.when(pl.program_id(1) == 0)
    def _():
        for n, w_ref in enumerate((wq_ref, wk_ref, wv_ref, wg_ref)):
            w_bf16_ref[n] = w_ref[...].astype(BF16)

    hn = hn_ref[...]
    dot = lambda n: jnp.dot(hn, w_bf16_ref[n], preferred_element_type=F32)
    q_ref[...] = (dot(0) * (SB_HEAD_DIM ** -0.5 * LOG2_E)).astype(BF16)
    k_ref[...] = dot(1).astype(BF16)
    v_ref[...] = dot(2).astype(BF16)
    sg_ref[...] = _silu(dot(3)).astype(BF16)


def _sb_in(hn, w_in, *, tm, tn):
    tokens, d_model = hn.shape
    width = w_in.shape[1] // 4
    nj = width // tn
    grid = (nj, tokens // tm)
    w_spec = lambda off: pl.BlockSpec((d_model, tn), lambda j, i, off=off: (0, j + off))
    out_spec = pl.BlockSpec((tm, tn), lambda j, i: (i, j))
    out_shape = jax.ShapeDtypeStruct((tokens, width), BF16)
    return pl.pallas_call(
        _sb_in_kernel,
        grid=grid,
        in_specs=[
            pl.BlockSpec((tm, d_model), lambda j, i: (i, 0)),
            w_spec(0), w_spec(nj), w_spec(2 * nj), w_spec(3 * nj),
        ],
        out_specs=[out_spec] * 4,
        out_shape=[out_shape] * 4,
        scratch_shapes=[pltpu.VMEM((4, d_model, tn), BF16)],
        compiler_params=_params(("parallel", "arbitrary")),
        name="sb_in_proj",
    )(hn, w_in, w_in, w_in, w_in)


def _sb_attn_kernel(q_ref, k_ref, v_ref, sg_ref, o_ref, *, tq, nh):
    qi = pl.program_id(2)
    row = lax.broadcasted_iota(jnp.int32, (tq, tq), 0)
    col = lax.broadcasted_iota(jnp.int32, (tq, tq), 1)
    suffix = (row > col).astype(BF16)
    causal = col < row
    lanes = [slice(h * SB_HEAD_DIM, (h + 1) * SB_HEAD_DIM) for h in range(nh)]
    qs = [q_ref[:, lanes[h]] for h in range(nh)]

    def block(kb, h, carry, masked):
        keys = pl.ds(pl.multiple_of(kb * tq, tq), tq)
        z = lax.dot_general(qs[h], k_ref[keys, lanes[h]], (((1,), (1,)), ((), ())),
                            preferred_element_type=F32)
        log1p_term = jnp.log2(1.0 + jnp.exp2(-jnp.abs(z)))
        log_beta = jnp.minimum(z, 0.0) - log1p_term
        log_fail = log_beta - z
        if masked:
            log_fail = jnp.where(causal, log_fail, 0.0)
        tail = jnp.dot(log_fail.astype(BF16), suffix, preferred_element_type=F32)
        a = jnp.exp2(log_beta + tail + carry)
        if masked:
            a = jnp.where(causal, a, 0.0)
        out = jnp.dot(a.astype(BF16), v_ref[keys, lanes[h]], preferred_element_type=F32)
        return jnp.sum(log_fail, axis=-1, keepdims=True), out

    prev = jnp.maximum(qi - 1, 0)
    no_prev = jnp.where(qi > 0, 0.0, NO_WEIGHT)
    state = []
    for h in range(nh):
        sum_d, out_d = block(qi, h, jnp.zeros((tq, 1), F32), masked=True)
        sum_p, out_p = block(prev, h, sum_d + no_prev, masked=False)
        state.append((sum_d + sum_p, out_d + out_p))

    def live(state):
        top = state[0][0]
        for h in range(1, nh):
            top = jnp.maximum(top, state[h][0])
        return jnp.max(top) >= ZERO_WEIGHT_LOG2

    def cond(loop_state):
        kb, alive, _ = loop_state
        return jnp.logical_and(kb >= 0, alive)

    def body(loop_state):
        kb, _, state = loop_state
        new_state = []
        for h in range(nh):
            carry, acc = state[h]
            block_sum, out = block(kb, h, carry, masked=False)
            new_state.append((carry + block_sum, acc + out))
        new_state = tuple(new_state)
        return kb - 1, live(new_state), new_state

    state = tuple(state)
    _, _, state = lax.while_loop(cond, body, (qi - 2, live(state), state))
    for h in range(nh):
        o_ref[:, lanes[h]] = (state[h][1] * sg_ref[:, lanes[h]].astype(F32)).astype(BF16)


def _sb_attn(q, k, v, sg, *, batch, seq, tq, nh):
    tokens, width = q.shape
    heads = width // SB_HEAD_DIM
    nq = seq // tq
    grid = (batch, heads // nh, nq)
    q_spec = pl.BlockSpec((tq, nh * SB_HEAD_DIM), lambda b, h, i: (b * nq + i, h))
    kv_spec = pl.BlockSpec((seq, nh * SB_HEAD_DIM), lambda b, h, i: (b, h))
    return pl.pallas_call(
        functools.partial(_sb_attn_kernel, tq=tq, nh=nh),
        grid=grid,
        in_specs=[q_spec, kv_spec, kv_spec, q_spec],
        out_specs=q_spec,
        out_shape=jax.ShapeDtypeStruct((tokens, width), BF16),
        compiler_params=_params(("parallel", "parallel", "arbitrary")),
        name="sb_attention",
    )(q, k, v, sg)


def _sb_out_kernel(h_ref, o_ref, wo_ref, g_ref, out_ref, wo_bf16_ref):
    @pl.when(pl.program_id(0) == 0)
    def _():
        wo_bf16_ref[...] = wo_ref[...].astype(BF16)

    h2 = h_ref[...] + jnp.dot(o_ref[...], wo_bf16_ref[...], preferred_element_type=F32)
    out_ref[...] = _rms_normalize(h2, g_ref[...])


def _sb_out(h, og, w_out, final_g, *, tm):
    tokens, d_model = h.shape
    width = og.shape[1]
    return pl.pallas_call(
        _sb_out_kernel,
        grid=(tokens // tm,),
        in_specs=[
            pl.BlockSpec((tm, d_model), lambda i: (i, 0)),
            pl.BlockSpec((tm, width), lambda i: (i, 0)),
            pl.BlockSpec((width, d_model), lambda i: (0, 0), pipeline_mode=pl.Buffered(1)),
            pl.BlockSpec((1, d_model), lambda i: (0, 0)),
        ],
        out_specs=pl.BlockSpec((tm, d_model), lambda i: (i, 0)),
        out_shape=jax.ShapeDtypeStruct((tokens, d_model), F32),
        scratch_shapes=[pltpu.VMEM((width, d_model), BF16)],
        compiler_params=_params(("arbitrary",)),
        name="sb_out_proj_norm",
    )(h, og, w_out, final_g)


def kernel(x, norm_g, a_w_in, a_v_norm_g, a_w_s, a_b_s, a_w_out, b_w_in, b_w_out, final_g):
    batch, seq, d_model = x.shape
    assert norm_g.shape[0] == 2 and a_w_in.shape[0] == 1 and b_w_in.shape[0] == 1
    x2 = x.reshape(batch * seq, d_model)
    ug, v, rinv = _gmlp_in(x2, norm_g[0][None, :], a_w_in[0], tm=1024, tn=512)
    h1, hn1 = _gmlp_out(x2, v, ug, rinv, a_v_norm_g[0][None, :], a_w_s[0], a_b_s[0][:, :, None],
                        a_w_out[0].astype(BF16), norm_g[1][None, :], tm=512, gk=4)
    q, k, v, sg = _sb_in(hn1, b_w_in[0], tm=1024, tn=256)
    og = _sb_attn(q, k, v, sg, batch=batch, seq=seq, tq=256, nh=4)
    out = _sb_out(h1, og, b_w_out[0], final_g[None, :], tm=512)
    return out.reshape(batch, seq, d_model)
```

```python
import functools

import jax
import jax.numpy as jnp
from jax import lax
from jax.experimental import pallas as pl
from jax.experimental.pallas import tpu as pltpu

EPS = 1e-6
CHUNK = 128
GMLP_GROUPS = 16
SB_HEAD_DIM = 128
GELU_C = 0.7978845608028654
LOG2_E = 1.4426950408889634
ZERO_WEIGHT_LOG2 = -160.0
NO_WEIGHT = -1e30

V7X_VMEM_LIMIT_BYTES = 56 * 1024 * 1024

BF16 = jnp.bfloat16
F32 = jnp.float32


def _gelu_tanh(x):
    return 0.5 * x * (1.0 + jnp.tanh(GELU_C * (x + 0.044715 * (x * x * x))))


def _silu(x):
    return 0.5 * x * (1.0 + jnp.tanh(0.5 * x))


def _rms_normalize(x_f32, gain_f32):
    ms = jnp.mean(x_f32 * x_f32, axis=-1, keepdims=True)
    return x_f32 * lax.rsqrt(ms + EPS) * gain_f32


def _dot_w(x_bf16, w_ref):
    return jnp.dot(x_bf16, w_ref[...].astype(BF16), preferred_element_type=F32)


def _params(semantics):
    return pltpu.CompilerParams(dimension_semantics=semantics,
                                vmem_limit_bytes=V7X_VMEM_LIMIT_BYTES)


def _gmlp_in_kernel(x_ref, g_ref, wu_ref, wv_ref, wg_ref, ug_ref, v_ref, rinv_ref,
                    xn_ref, ssq_ref, *, width):
    j = pl.program_id(1)

    @pl.when(j == 0)
    def _():
        xn_ref[...] = _rms_normalize(x_ref[...], g_ref[...]).astype(BF16)
        ssq_ref[...] = jnp.zeros_like(ssq_ref)

    xn = xn_ref[...]
    u = _dot_w(xn, wu_ref)
    v = _dot_w(xn, wv_ref)
    zg = _dot_w(xn, wg_ref)
    v = _gelu_tanh(v)
    ug_ref[...] = (_gelu_tanh(u) * _silu(zg)).astype(BF16)
    v_ref[...] = v.astype(BF16)
    ssq_ref[...] += jnp.sum(v * v, axis=-1, keepdims=True)

    @pl.when(j == pl.num_programs(1) - 1)
    def _():
        rinv_ref[...] = lax.rsqrt(ssq_ref[...] * (1.0 / width) + EPS)


def _gmlp_in(x2, gain, w_in, *, tm, tn):
    tokens, d_model = x2.shape
    width = w_in.shape[1] // 3
    nj = width // tn
    grid = (tokens // tm, nj)
    w_spec = lambda off: pl.BlockSpec((d_model, tn), lambda i, j, off=off: (0, j + off))
    return pl.pallas_call(
        functools.partial(_gmlp_in_kernel, width=width),
        grid=grid,
        in_specs=[
            pl.BlockSpec((tm, d_model), lambda i, j: (i, 0)),
            pl.BlockSpec((1, d_model), lambda i, j: (0, 0)),
            w_spec(0), w_spec(nj), w_spec(2 * nj),
        ],
        out_specs=[
            pl.BlockSpec((tm, tn), lambda i, j: (i, j)),
            pl.BlockSpec((tm, tn), lambda i, j: (i, j)),
            pl.BlockSpec((tm, 1), lambda i, j: (i, 0)),
        ],
        out_shape=[
            jax.ShapeDtypeStruct((tokens, width), BF16),
            jax.ShapeDtypeStruct((tokens, width), BF16),
            jax.ShapeDtypeStruct((tokens, 1), F32),
        ],
        scratch_shapes=[pltpu.VMEM((tm, d_model), BF16), pltpu.VMEM((tm, 1), F32)],
        compiler_params=_params(("parallel", "arbitrary")),
        name="gmlp_in_proj",
    )(x2, gain, w_in, w_in, w_in)


def _gmlp_out_kernel(x_ref, v_ref, ug_ref, rinv_ref, vg_ref, ws_ref, bs_ref, wo_ref, ng_ref,
                     h_ref, hn_ref, *, tm, gk, gd):
    step = pl.program_id(1)

    @pl.when(step == 0)
    def _():
        h_ref[...] = x_ref[...]

    row = lax.broadcasted_iota(jnp.int32, (CHUNK, CHUNK), 0)
    col = lax.broadcasted_iota(jnp.int32, (CHUNK, CHUNK), 1)
    rinv = rinv_ref[...]
    ys = []
    for g in range(gk):
        cols = slice(g * gd, (g + 1) * gd)
        ws = jnp.where(row >= col, ws_ref[g], 0.0).astype(BF16)
        bias = bs_ref[g]
        vn = (v_ref[:, cols].astype(F32) * rinv * vg_ref[:, cols]).astype(BF16)
        yg = []
        for c in range(tm // CHUNK):
            rows = slice(c * CHUNK, (c + 1) * CHUNK)
            mixed = jnp.dot(ws, vn[rows], preferred_element_type=F32) + bias
            yg.append((ug_ref[rows, cols].astype(F32) * mixed).astype(BF16))
        ys.append(jnp.concatenate(yg, axis=0))
    y = jnp.concatenate(ys, axis=1)
    h_ref[...] += jnp.dot(y, wo_ref[...], preferred_element_type=F32)

    @pl.when(step == pl.num_programs(1) - 1)
    def _():
        hn_ref[...] = _rms_normalize(h_ref[...], ng_ref[...]).astype(BF16)


def _gmlp_out(x2, v, ug, rinv, v_gain, w_s, b_s3, w_out_bf16, next_gain, *, tm, gk):
    tokens, d_model = x2.shape
    width = v.shape[1]
    gd = width // GMLP_GROUPS
    grid = (tokens // tm, GMLP_GROUPS // gk)
    return pl.pallas_call(
        functools.partial(_gmlp_out_kernel, tm=tm, gk=gk, gd=gd),
        grid=grid,
        in_specs=[
            pl.BlockSpec((tm, d_model), lambda i, g: (i, 0)),
            pl.BlockSpec((tm, gk * gd), lambda i, g: (i, g)),
            pl.BlockSpec((tm, gk * gd), lambda i, g: (i, g)),
            pl.BlockSpec((tm, 1), lambda i, g: (i, 0)),
            pl.BlockSpec((1, gk * gd), lambda i, g: (0, g)),
            pl.BlockSpec((gk, CHUNK, CHUNK), lambda i, g: (g, 0, 0)),
            pl.BlockSpec((gk, CHUNK, 1), lambda i, g: (g, 0, 0)),
            pl.BlockSpec((gk * gd, d_model), lambda i, g: (g, 0)),
            pl.BlockSpec((1, d_model), lambda i, g: (0, 0)),
        ],
        out_specs=[pl.BlockSpec((tm, d_model), lambda i, g: (i, 0))] * 2,
        out_shape=[jax.ShapeDtypeStruct((tokens, d_model), F32),
                   jax.ShapeDtypeStruct((tokens, d_model), BF16)],
        compiler_params=_params(("parallel", "arbitrary")),
        name="gmlp_mix_out_proj",
    )(x2, v, ug, rinv, v_gain, w_s, b_s3, w_out_bf16, next_gain)


def _sb_in_kernel(hn_ref, wq_ref, wk_ref, wv_ref, wg_ref, q_ref, k_ref, v_ref, sg_ref, w_bf16_ref):
    @pl.when(pl.program_id(1) == 0)
    def _():
        for n, w_ref in enumerate((wq_ref, wk_ref, wv_ref, wg_ref)):
            w_bf16_ref[n] = w_ref[...].astype(BF16)

    hn = hn_ref[...]
    dot = lambda n: jnp.dot(hn, w_bf16_ref[n], preferred_element_type=F32)
    q_ref[...] = (dot(0) * (SB_HEAD_DIM ** -0.5 * LOG2_E)).astype(BF16)
    k_ref[...] = dot(1).astype(BF16)
    v_ref[...] = dot(2).astype(BF16)
    sg_ref[...] = _silu(dot(3)).astype(BF16)


def _sb_in(hn, w_in, *, tm, tn):
    tokens, d_model = hn.shape
    width = w_in.shape[1] // 4
    nj = width // tn
    grid = (nj, tokens // tm)
    w_spec = lambda off: pl.BlockSpec((d_model, tn), lambda j, i, off=off: (0, j + off))
    out_spec = pl.BlockSpec((tm, tn), lambda j, i: (i, j))
    out_shape = jax.ShapeDtypeStruct((tokens, width), BF16)
    return pl.pallas_call(
        _sb_in_kernel,
        grid=grid,
        in_specs=[
            pl.BlockSpec((tm, d_model), lambda j, i: (i, 0)),
            w_spec(0), w_spec(nj), w_spec(2 * nj), w_spec(3 * nj),
        ],
        out_specs=[out_spec] * 4,
        out_shape=[out_shape] * 4,
        scratch_shapes=[pltpu.VMEM((4, d_model, tn), BF16)],
        compiler_params=_params(("parallel", "arbitrary")),
        name="sb_in_proj",
    )(hn, w_in, w_in, w_in, w_in)


def _sb_attn_kernel(q_ref, k_ref, v_ref, sg_ref, o_ref, *, tq, nh):
    qi = pl.program_id(2)
    row = lax.broadcasted_iota(jnp.int32, (tq, tq), 0)
    col = lax.broadcasted_iota(jnp.int32, (tq, tq), 1)
    suffix = (row > col).astype(BF16)
    causal = col < row
    lanes = [slice(h * SB_HEAD_DIM, (h + 1) * SB_HEAD_DIM) for h in range(nh)]
    qs = [q_ref[:, lanes[h]] for h in range(nh)]

    def block(kb, h, carry, masked):
        keys = pl.ds(pl.multiple_of(kb * tq, tq), tq)
        z = lax.dot_general(qs[h], k_ref[keys, lanes[h]], (((1,), (1,)), ((), ())),
                            preferred_element_type=F32)
        log1p_term = jnp.log2(1.0 + jnp.exp2(-jnp.abs(z)))
        log_beta = jnp.minimum(z, 0.0) - log1p_term
        log_fail = log_beta - z
        if masked:
            log_fail = jnp.where(causal, log_fail, 0.0)
        tail = jnp.dot(log_fail.astype(BF16), suffix, preferred_element_type=F32)
        a = jnp.exp2(log_beta + tail + carry)
        if masked:
            a = jnp.where(causal, a, 0.0)
        out = jnp.dot(a.astype(BF16), v_ref[keys, lanes[h]], preferred_element_type=F32)
        return jnp.sum(log_fail, axis=-1, keepdims=True), out

    prev = jnp.maximum(qi - 1, 0)
    no_prev = jnp.where(qi > 0, 0.0, NO_WEIGHT)
    state = []
    for h in range(nh):
        sum_d, out_d = block(qi, h, jnp.zeros((tq, 1), F32), masked=True)
        sum_p, out_p = block(prev, h, sum_d + no_prev, masked=False)
        state.append((sum_d + sum_p, out_d + out_p))

    def live(state):
        top = state[0][0]
        for h in range(1, nh):
            top = jnp.maximum(top, state[h][0])
        return jnp.max(top) >= ZERO_WEIGHT_LOG2

    def cond(loop_state):
        kb, alive, _ = loop_state
        return jnp.logical_and(kb >= 0, alive)

    def body(loop_state):
        kb, _, state = loop_state
        new_state = []
        for h in range(nh):
            carry, acc = state[h]
            block_sum, out = block(kb, h, carry, masked=False)
            new_state.append((carry + block_sum, acc + out))
        new_state = tuple(new_state)
        return kb - 1, live(new_state), new_state

    state = tuple(state)
    _, _, state = lax.while_loop(cond, body, (qi - 2, live(state), state))
    for h in range(nh):
        o_ref[:, lanes[h]] = (state[h][1] * sg_ref[:, lanes[h]].astype(F32)).astype(BF16)


def _sb_attn(q, k, v, sg, *, batch, seq, tq, nh):
    tokens, width = q.shape
    heads = width // SB_HEAD_DIM
    nq = seq // tq
    grid = (batch, heads // nh, nq)
    q_spec = pl.BlockSpec((tq, nh * SB_HEAD_DIM), lambda b, h, i: (b * nq + i, h))
    kv_spec = pl.BlockSpec((seq, nh * SB_HEAD_DIM), lambda b, h, i: (b, h))
    return pl.pallas_call(
        functools.partial(_sb_attn_kernel, tq=tq, nh=nh),
        grid=grid,
        in_specs=[q_spec, kv_spec, kv_spec, q_spec],
        out_specs=q_spec,
        out_shape=jax.ShapeDtypeStruct((tokens, width), BF16),
        compiler_params=_params(("parallel", "parallel", "arbitrary")),
        name="sb_attention",
    )(q, k, v, sg)


def _sb_out_kernel(h_ref, o_ref, wo_ref, g_ref, out_ref, wo_bf16_ref):
    @pl.when(pl.program_id(0) == 0)
    def _():
        wo_bf16_ref[...] = wo_ref[...].astype(BF16)

    h2 = h_ref[...] + jnp.dot(o_ref[...], wo_bf16_ref[...], preferred_element_type=F32)
    out_ref[...] = _rms_normalize(h2, g_ref[...])


def _sb_out(h, og, w_out, final_g, *, tm):
    tokens, d_model = h.shape
    width = og.shape[1]
    return pl.pallas_call(
        _sb_out_kernel,
        grid=(tokens // tm,),
        in_specs=[
            pl.BlockSpec((tm, d_model), lambda i: (i, 0)),
            pl.BlockSpec((tm, width), lambda i: (i, 0)),
            pl.BlockSpec((width, d_model), lambda i: (0, 0), pipeline_mode=pl.Buffered(1)),
            pl.BlockSpec((1, d_model), lambda i: (0, 0)),
        ],
        out_specs=pl.BlockSpec((tm, d_model), lambda i: (i, 0)),
        out_shape=jax.ShapeDtypeStruct((tokens, d_model), F32),
        scratch_shapes=[pltpu.VMEM((width, d_model), BF16)],
        compiler_params=_params(("arbitrary",)),
        name="sb_out_proj_norm",
    )(h, og, w_out, final_g)


def kernel(x, norm_g, a_w_in, a_v_norm_g, a_w_s, a_b_s, a_w_out, b_w_in, b_w_out, final_g):
    batch, seq, d_model = x.shape
    assert norm_g.shape[0] == 2 and a_w_in.shape[0] == 1 and b_w_in.shape[0] == 1
    x2 = x.reshape(batch * seq, d_model)
    ug, v, rinv = _gmlp_in(x2, norm_g[0][None, :], a_w_in[0], tm=1024, tn=512)
    h1, hn1 = _gmlp_out(x2, v, ug, rinv, a_v_norm_g[0][None, :], a_w_s[0], a_b_s[0][:, :, None],
                        a_w_out[0].astype(BF16), norm_g[1][None, :], tm=1024, gk=2)
    q, k, v, sg = _sb_in(hn1, b_w_in[0], tm=1024, tn=256)
    og = _sb_attn(q, k, v, sg, batch=batch, seq=seq, tq=256, nh=4)
    out = _sb_out(h1, og, b_w_out[0], final_g[None, :], tm=512)
    return out.reshape(batch, seq, d_model)
```

```python
import functools

import jax
import jax.numpy as jnp
from jax import lax
from jax.experimental import pallas as pl
from jax.experimental.pallas import tpu as pltpu

EPS = 1e-6
CHUNK = 128
GMLP_GROUPS = 16
SB_HEAD_DIM = 128
GELU_C = 0.7978845608028654
LOG2_E = 1.4426950408889634
ZERO_WEIGHT_LOG2 = -160.0
NO_WEIGHT = -1e30

V7X_VMEM_LIMIT_BYTES = 56 * 1024 * 1024

BF16 = jnp.bfloat16
F32 = jnp.float32


def _gelu_tanh(x):
    return 0.5 * x * (1.0 + jnp.tanh(GELU_C * (x + 0.044715 * (x * x * x))))


def _silu(x):
    return 0.5 * x * (1.0 + jnp.tanh(0.5 * x))


def _rms_normalize(x_f32, gain_f32):
    ms = jnp.mean(x_f32 * x_f32, axis=-1, keepdims=True)
    return x_f32 * lax.rsqrt(ms + EPS) * gain_f32


def _dot_w(x_bf16, w_ref):
    return jnp.dot(x_bf16, w_ref[...].astype(BF16), preferred_element_type=F32)


def _params(semantics):
    return pltpu.CompilerParams(dimension_semantics=semantics,
                                vmem_limit_bytes=V7X_VMEM_LIMIT_BYTES)


def _gmlp_in_kernel(x_ref, g_ref, wu_ref, wv_ref, wg_ref, ug_ref, v_ref, rinv_ref,
                    xn_ref, ssq_ref, *, width):
    j = pl.program_id(1)

    @pl.when(j == 0)
    def _():
        xn_ref[...] = _rms_normalize(x_ref[...], g_ref[...]).astype(BF16)
        ssq_ref[...] = jnp.zeros_like(ssq_ref)

    xn = xn_ref[...]
    u = _dot_w(xn, wu_ref)
    v = _dot_w(xn, wv_ref)
    zg = _dot_w(xn, wg_ref)
    v = _gelu_tanh(v)
    ug_ref[...] = (_gelu_tanh(u) * _silu(zg)).astype(BF16)
    v_ref[...] = v.astype(BF16)
    ssq_ref[...] += jnp.sum(v * v, axis=-1, keepdims=True)

    @pl.when(j == pl.num_programs(1) - 1)
    def _():
        rinv_ref[...] = lax.rsqrt(ssq_ref[...] * (1.0 / width) + EPS)


def _gmlp_in(x2, gain, w_in, *, tm, tn):
    tokens, d_model = x2.shape
    width = w_in.shape[1] // 3
    nj = width // tn
    grid = (tokens // tm, nj)
    w_spec = lambda off: pl.BlockSpec((d_model, tn), lambda i, j, off=off: (0, j + off))
    return pl.pallas_call(
        functools.partial(_gmlp_in_kernel, width=width),
        grid=grid,
        in_specs=[
            pl.BlockSpec((tm, d_model), lambda i, j: (i, 0)),
            pl.BlockSpec((1, d_model), lambda i, j: (0, 0)),
            w_spec(0), w_spec(nj), w_spec(2 * nj),
        ],
        out_specs=[
            pl.BlockSpec((tm, tn), lambda i, j: (i, j)),
            pl.BlockSpec((tm, tn), lambda i, j: (i, j)),
            pl.BlockSpec((tm, 1), lambda i, j: (i, 0)),
        ],
        out_shape=[
            jax.ShapeDtypeStruct((tokens, width), BF16),
            jax.ShapeDtypeStruct((tokens, width), BF16),
            jax.ShapeDtypeStruct((tokens, 1), F32),
        ],
        scratch_shapes=[pltpu.VMEM((tm, d_model), BF16), pltpu.VMEM((tm, 1), F32)],
        compiler_params=_params(("parallel", "arbitrary")),
        name="gmlp_in_proj",
    )(x2, gain, w_in, w_in, w_in)


def _gmlp_out_kernel(x_ref, v_ref, ug_ref, rinv_ref, vg_ref, ws_ref, bs_ref, wo_ref, ng_ref,
                     h_ref, hn_ref, y_ref, *, tm, gd):
    row = lax.broadcasted_iota(jnp.int32, (CHUNK, CHUNK), 0)
    col = lax.broadcasted_iota(jnp.int32, (CHUNK, CHUNK), 1)
    rinv = rinv_ref[...]
    for g in range(GMLP_GROUPS):
        cols = slice(g * gd, (g + 1) * gd)
        ws = jnp.where(row >= col, ws_ref[g], 0.0).astype(BF16)
        bias = bs_ref[g]
        vn = (v_ref[:, cols].astype(F32) * rinv * vg_ref[:, cols]).astype(BF16)
        for c in range(tm // CHUNK):
            rows = slice(c * CHUNK, (c + 1) * CHUNK)
            mixed = jnp.dot(ws, vn[rows], preferred_element_type=F32) + bias
            y_ref[rows, cols] = (ug_ref[rows, cols].astype(F32) * mixed).astype(BF16)
    h_ref[...] = x_ref[...] + jnp.dot(y_ref[...], wo_ref[...], preferred_element_type=F32)
    hn_ref[...] = _rms_normalize(h_ref[...], ng_ref[...]).astype(BF16)


def _gmlp_out(x2, v, ug, rinv, v_gain, w_s, b_s3, w_out_bf16, next_gain, *, tm):
    tokens, d_model = x2.shape
    width = v.shape[1]
    gd = width // GMLP_GROUPS
    row_tile = lambda cols: pl.BlockSpec((tm, cols), lambda i: (i, 0))
    whole = lambda shape: pl.BlockSpec(shape, lambda i: (0,) * len(shape))
    return pl.pallas_call(
        functools.partial(_gmlp_out_kernel, tm=tm, gd=gd),
        grid=(tokens // tm,),
        in_specs=[
            row_tile(d_model), row_tile(width), row_tile(width), row_tile(1),
            whole((1, width)), whole((GMLP_GROUPS, CHUNK, CHUNK)), whole((GMLP_GROUPS, CHUNK, 1)),
            pl.BlockSpec((width, d_model), lambda i: (0, 0), pipeline_mode=pl.Buffered(1)),
            whole((1, d_model)),
        ],
        out_specs=[row_tile(d_model)] * 2,
        out_shape=[jax.ShapeDtypeStruct((tokens, d_model), F32),
                   jax.ShapeDtypeStruct((tokens, d_model), BF16)],
        scratch_shapes=[pltpu.VMEM((tm, width), BF16)],
        compiler_params=_params(("parallel",)),
        name="gmlp_mix_out_proj",
    )(x2, v, ug, rinv, v_gain, w_s, b_s3, w_out_bf16, next_gain)


def _sb_in_kernel(hn_ref, wq_ref, wk_ref, wv_ref, wg_ref, q_ref, k_ref, v_ref, sg_ref, w_bf16_ref):
    @pl.when(pl.program_id(1) == 0)
    def _():
        for n, w_ref in enumerate((wq_ref, wk_ref, wv_ref, wg_ref)):
            w_bf16_ref[n] = w_ref[...].astype(BF16)

    hn = hn_ref[...]
    dot = lambda n: jnp.dot(hn, w_bf16_ref[n], preferred_element_type=F32)
    q_ref[...] = (dot(0) * (SB_HEAD_DIM ** -0.5 * LOG2_E)).astype(BF16)
    k_ref[...] = dot(1).astype(BF16)
    v_ref[...] = dot(2).astype(BF16)
    sg_ref[...] = _silu(dot(3)).astype(BF16)


def _sb_in(hn, w_in, *, tm, tn):
    tokens, d_model = hn.shape
    width = w_in.shape[1] // 4
    nj = width // tn
    grid = (nj, tokens // tm)
    w_spec = lambda off: pl.BlockSpec((d_model, tn), lambda j, i, off=off: (0, j + off))
    out_spec = pl.BlockSpec((tm, tn), lambda j, i: (i, j))
    out_shape = jax.ShapeDtypeStruct((tokens, width), BF16)
    return pl.pallas_call(
        _sb_in_kernel,
        grid=grid,
        in_specs=[
            pl.BlockSpec((tm, d_model), lambda j, i: (i, 0)),
            w_spec(0), w_spec(nj), w_spec(2 * nj), w_spec(3 * nj),
        ],
        out_specs=[out_spec] * 4,
        out_shape=[out_shape] * 4,
        scratch_shapes=[pltpu.VMEM((4, d_model, tn), BF16)],
        compiler_params=_params(("parallel", "arbitrary")),
        name="sb_in_proj",
    )(hn, w_in, w_in, w_in, w_in)


def _sb_attn_kernel(q_ref, k_ref, v_ref, sg_ref, o_ref, *, tq, nh):
    qi = pl.program_id(2)
    row = lax.broadcasted_iota(jnp.int32, (tq, tq), 0)
    col = lax.broadcasted_iota(jnp.int32, (tq, tq), 1)
    suffix = (row > col).astype(BF16)
    causal = col < row
    lanes = [slice(h * SB_HEAD_DIM, (h + 1) * SB_HEAD_DIM) for h in range(nh)]
    qs = [q_ref[:, lanes[h]] for h in range(nh)]

    def block(kb, h, carry, masked):
        keys = pl.ds(pl.multiple_of(kb * tq, tq), tq)
        z = lax.dot_general(qs[h], k_ref[keys, lanes[h]], (((1,), (1,)), ((), ())),
                            preferred_element_type=F32)
        log1p_term = jnp.log2(1.0 + jnp.exp2(-jnp.abs(z)))
        log_beta = jnp.minimum(z, 0.0) - log1p_term
        log_fail = log_beta - z
        if masked:
            log_fail = jnp.where(causal, log_fail, 0.0)
        tail = jnp.dot(log_fail.astype(BF16), suffix, preferred_element_type=F32)
        a = jnp.exp2(log_beta + tail + carry)
        if masked:
            a = jnp.where(causal, a, 0.0)
        out = jnp.dot(a.astype(BF16), v_ref[keys, lanes[h]], preferred_element_type=F32)
        return jnp.sum(log_fail, axis=-1, keepdims=True), out

    prev = jnp.maximum(qi - 1, 0)
    no_prev = jnp.where(qi > 0, 0.0, NO_WEIGHT)
    state = []
    for h in range(nh):
        sum_d, out_d = block(qi, h, jnp.zeros((tq, 1), F32), masked=True)
        sum_p, out_p = block(prev, h, sum_d + no_prev, masked=False)
        state.append((sum_d + sum_p, out_d + out_p))

    def live(state):
        top = state[0][0]
        for h in range(1, nh):
            top = jnp.maximum(top, state[h][0])
        return jnp.max(top) >= ZERO_WEIGHT_LOG2

    def cond(loop_state):
        kb, alive, _ = loop_state
        return jnp.logical_and(kb >= 0, alive)

    def body(loop_state):
        kb, _, state = loop_state
        new_state = []
        for h in range(nh):
            carry, acc = state[h]
            block_sum, out = block(kb, h, carry, masked=False)
            new_state.append((carry + block_sum, acc + out))
        new_state = tuple(new_state)
        return kb - 1, live(new_state), new_state

    state = tuple(state)
    _, _, state = lax.while_loop(cond, body, (qi - 2, live(state), state))
    for h in range(nh):
        o_ref[:, lanes[h]] = (state[h][1] * sg_ref[:, lanes[h]].astype(F32)).astype(BF16)


def _sb_attn(q, k, v, sg, *, batch, seq, tq, nh):
    tokens, width = q.shape
    heads = width // SB_HEAD_DIM
    nq = seq // tq
    grid = (batch, heads // nh, nq)
    q_spec = pl.BlockSpec((tq, nh * SB_HEAD_DIM), lambda b, h, i: (b * nq + i, h))
    kv_spec = pl.BlockSpec((seq, nh * SB_HEAD_DIM), lambda b, h, i: (b, h))
    return pl.pallas_call(
        functools.partial(_sb_attn_kernel, tq=tq, nh=nh),
        grid=grid,
        in_specs=[q_spec, kv_spec, kv_spec, q_spec],
        out_specs=q_spec,
        out_shape=jax.ShapeDtypeStruct((tokens, width), BF16),
        compiler_params=_params(("parallel", "parallel", "arbitrary")),
        name="sb_attention",
    )(q, k, v, sg)


def _sb_out_kernel(h_ref, o_ref, wo_ref, g_ref, out_ref, wo_bf16_ref):
    @pl.when(pl.program_id(0) == 0)
    def _():
        wo_bf16_ref[...] = wo_ref[...].astype(BF16)

    h2 = h_ref[...] + jnp.dot(o_ref[...], wo_bf16_ref[...], preferred_element_type=F32)
    out_ref[...] = _rms_normalize(h2, g_ref[...])


def _sb_out(h, og, w_out, final_g, *, tm):
    tokens, d_model = h.shape
    width = og.shape[1]
    return pl.pallas_call(
        _sb_out_kernel,
        grid=(tokens // tm,),
        in_specs=[
            pl.BlockSpec((tm, d_model), lambda i: (i, 0)),
            pl.BlockSpec((tm, width), lambda i: (i, 0)),
            pl.BlockSpec((width, d_model), lambda i: (0, 0), pipeline_mode=pl.Buffered(1)),
            pl.BlockSpec((1, d_model), lambda i: (0, 0)),
        ],
        out_specs=pl.BlockSpec((tm, d_model), lambda i: (i, 0)),
        out_shape=jax.ShapeDtypeStruct((tokens, d_model), F32),
        scratch_shapes=[pltpu.VMEM((width, d_model), BF16)],
        compiler_params=_params(("arbitrary",)),
        name="sb_out_proj_norm",
    )(h, og, w_out, final_g)


def kernel(x, norm_g, a_w_in, a_v_norm_g, a_w_s, a_b_s, a_w_out, b_w_in, b_w_out, final_g):
    batch, seq, d_model = x.shape
    assert norm_g.shape[0] == 2 and a_w_in.shape[0] == 1 and b_w_in.shape[0] == 1
    x2 = x.reshape(batch * seq, d_model)
    ug, v, rinv = _gmlp_in(x2, norm_g[0][None, :], a_w_in[0], tm=1024, tn=512)
    h1, hn1 = _gmlp_out(x2, v, ug, rinv, a_v_norm_g[0][None, :], a_w_s[0], a_b_s[0][:, :, None],
                        a_w_out[0].astype(BF16), norm_g[1][None, :], tm=256)
    q, k, v, sg = _sb_in(hn1, b_w_in[0], tm=1024, tn=256)
    og = _sb_attn(q, k, v, sg, batch=batch, seq=seq, tq=256, nh=4)
    out = _sb_out(h1, og, b_w_out[0], final_g[None, :], tm=512)
    return out.reshape(batch, seq, d_model)
```

```python
import functools

import jax
import jax.numpy as jnp
from jax import lax
from jax.experimental import pallas as pl
from jax.experimental.pallas import tpu as pltpu

EPS = 1e-6
CHUNK = 128
GMLP_GROUPS = 16
SB_HEAD_DIM = 128
GELU_C = 0.7978845608028654
LOG2_E = 1.4426950408889634
ZERO_WEIGHT_LOG2 = -160.0
NO_WEIGHT = -1e30
STAGE_SKEW = 1

V7X_VMEM_LIMIT_BYTES = 56 * 1024 * 1024

BF16 = jnp.bfloat16
F32 = jnp.float32


def _gelu_tanh(x):
    return 0.5 * x * (1.0 + jnp.tanh(GELU_C * (x + 0.044715 * (x * x * x))))


def _silu(x):
    return 0.5 * x * (1.0 + jnp.tanh(0.5 * x))


def _rms_normalize(x_f32, gain_f32):
    ms = jnp.mean(x_f32 * x_f32, axis=-1, keepdims=True)
    return x_f32 * lax.rsqrt(ms + EPS) * gain_f32


def _dot_w(x_bf16, w_ref):
    return jnp.dot(x_bf16, w_ref[...].astype(BF16), preferred_element_type=F32)


def _params(semantics):
    return pltpu.CompilerParams(dimension_semantics=semantics,
                                vmem_limit_bytes=V7X_VMEM_LIMIT_BYTES)


def _gmlp_in_kernel(x_ref, g_ref, wu_ref, wv_ref, wg_ref, ug_ref, v_ref, rinv_ref,
                    xn_ref, ssq_ref, *, width):
    j = pl.program_id(1)

    @pl.when(j == 0)
    def _():
        xn_ref[...] = _rms_normalize(x_ref[...], g_ref[...]).astype(BF16)
        ssq_ref[...] = jnp.zeros_like(ssq_ref)

    xn = xn_ref[...]
    u = _dot_w(xn, wu_ref)
    v = _dot_w(xn, wv_ref)
    zg = _dot_w(xn, wg_ref)
    v = _gelu_tanh(v)
    ug_ref[...] = (_gelu_tanh(u) * _silu(zg)).astype(BF16)
    v_ref[...] = v.astype(BF16)
    ssq_ref[...] += jnp.sum(v * v, axis=-1, keepdims=True)

    @pl.when(j == pl.num_programs(1) - 1)
    def _():
        rinv_ref[...] = lax.rsqrt(ssq_ref[...] * (1.0 / width) + EPS)


def _gmlp_in(x2, gain, w_in, *, tm, tn):
    tokens, d_model = x2.shape
    width = w_in.shape[1] // 3
    nj = width // tn
    grid = (tokens // tm, nj)
    w_spec = lambda off: pl.BlockSpec((d_model, tn), lambda i, j, off=off: (0, j + off))
    return pl.pallas_call(
        functools.partial(_gmlp_in_kernel, width=width),
        grid=grid,
        in_specs=[
            pl.BlockSpec((tm, d_model), lambda i, j: (i, 0)),
            pl.BlockSpec((1, d_model), lambda i, j: (0, 0)),
            w_spec(0), w_spec(nj), w_spec(2 * nj),
        ],
        out_specs=[
            pl.BlockSpec((tm, tn), lambda i, j: (i, j)),
            pl.BlockSpec((tm, tn), lambda i, j: (i, j)),
            pl.BlockSpec((tm, 1), lambda i, j: (i, 0)),
        ],
        out_shape=[
            jax.ShapeDtypeStruct((tokens, width), BF16),
            jax.ShapeDtypeStruct((tokens, width), BF16),
            jax.ShapeDtypeStruct((tokens, 1), F32),
        ],
        scratch_shapes=[pltpu.VMEM((tm, d_model), BF16), pltpu.VMEM((tm, 1), F32)],
        compiler_params=_params(("parallel", "arbitrary")),
        name="gmlp_in_proj",
    )(x2, gain, w_in, w_in, w_in)


def _gmlp_out_kernel(x_ref, v_ref, ug_ref, rinv_ref, vg_ref, ws_ref, bs_ref, wo_ref, ng_ref,
                     h_ref, hn_ref, y_ref, *, tm, gd):
    row = lax.broadcasted_iota(jnp.int32, (CHUNK, CHUNK), 0)
    col = lax.broadcasted_iota(jnp.int32, (CHUNK, CHUNK), 1)
    rinv = rinv_ref[...]
    for g in range(GMLP_GROUPS):
        cols = slice(g * gd, (g + 1) * gd)
        ws = jnp.where(row >= col, ws_ref[g], 0.0).astype(BF16)
        bias = bs_ref[g]
        vn = (v_ref[:, cols].astype(F32) * rinv * vg_ref[:, cols]).astype(BF16)
        for c in range(tm // CHUNK):
            rows = slice(c * CHUNK, (c + 1) * CHUNK)
            mixed = jnp.dot(ws, vn[rows], preferred_element_type=F32) + bias
            y_ref[rows, cols] = (ug_ref[rows, cols].astype(F32) * mixed).astype(BF16)
    h_ref[...] = x_ref[...] + jnp.dot(y_ref[...], wo_ref[...], preferred_element_type=F32)
    hn_ref[...] = _rms_normalize(h_ref[...], ng_ref[...]).astype(BF16)


def _gmlp_out(x2, v, ug, rinv, v_gain, w_s, b_s3, w_out_bf16, next_gain, *, tm):
    tokens, d_model = x2.shape
    width = v.shape[1]
    gd = width // GMLP_GROUPS
    row_tile = lambda cols: pl.BlockSpec((tm, cols), lambda i: (i, 0))
    whole = lambda shape: pl.BlockSpec(shape, lambda i: (0,) * len(shape))
    return pl.pallas_call(
        functools.partial(_gmlp_out_kernel, tm=tm, gd=gd),
        grid=(tokens // tm,),
        in_specs=[
            row_tile(d_model), row_tile(width), row_tile(width), row_tile(1),
            whole((1, width)), whole((GMLP_GROUPS, CHUNK, CHUNK)), whole((GMLP_GROUPS, CHUNK, 1)),
            pl.BlockSpec((width, d_model), lambda i: (0, 0), pipeline_mode=pl.Buffered(1)),
            whole((1, d_model)),
        ],
        out_specs=[row_tile(d_model)] * 2,
        out_shape=[jax.ShapeDtypeStruct((tokens, d_model), F32),
                   jax.ShapeDtypeStruct((tokens, d_model), BF16)],
        scratch_shapes=[pltpu.VMEM((tm, width), BF16)],
        compiler_params=_params(("parallel",)),
        name="gmlp_mix_out_proj",
    )(x2, v, ug, rinv, v_gain, w_s, b_s3, w_out_bf16, next_gain)


def _sb_in_kernel(hn_ref, wq_ref, wk_ref, wv_ref, wg_ref, q_ref, k_ref, v_ref, sg_ref, w_bf16_ref):
    @pl.when(pl.program_id(1) == 0)
    def _():
        for n, w_ref in enumerate((wq_ref, wk_ref, wv_ref, wg_ref)):
            w_bf16_ref[n] = w_ref[...].astype(BF16)

    hn = hn_ref[...]
    dot = lambda n: jnp.dot(hn, w_bf16_ref[n], preferred_element_type=F32)
    q_ref[...] = (dot(0) * (SB_HEAD_DIM ** -0.5 * LOG2_E)).astype(BF16)
    k_ref[...] = dot(1).astype(BF16)
    v_ref[...] = dot(2).astype(BF16)
    sg_ref[...] = _silu(dot(3)).astype(BF16)


def _sb_in(hn, w_in, *, tm, tn):
    tokens, d_model = hn.shape
    width = w_in.shape[1] // 4
    nj = width // tn
    grid = (nj, tokens // tm)
    w_spec = lambda off: pl.BlockSpec((d_model, tn), lambda j, i, off=off: (0, j + off))
    out_spec = pl.BlockSpec((tm, tn), lambda j, i: (i, j))
    out_shape = jax.ShapeDtypeStruct((tokens, width), BF16)
    return pl.pallas_call(
        _sb_in_kernel,
        grid=grid,
        in_specs=[
            pl.BlockSpec((tm, d_model), lambda j, i: (i, 0)),
            w_spec(0), w_spec(nj), w_spec(2 * nj), w_spec(3 * nj),
        ],
        out_specs=[out_spec] * 4,
        out_shape=[out_shape] * 4,
        scratch_shapes=[pltpu.VMEM((4, d_model, tn), BF16)],
        compiler_params=_params(("parallel", "arbitrary")),
        name="sb_in_proj",
    )(hn, w_in, w_in, w_in, w_in)


def _sb_attn_kernel(q_ref, k_ref, v_ref, sg_ref, o_ref, *, tq, nh):
    qi = pl.program_id(2)
    row = lax.broadcasted_iota(jnp.int32, (tq, tq), 0)
    col = lax.broadcasted_iota(jnp.int32, (tq, tq), 1)
    suffix = (row > col).astype(BF16)
    causal = col < row
    lanes = [slice(h * SB_HEAD_DIM, (h + 1) * SB_HEAD_DIM) for h in range(nh)]
    qs = [q_ref[:, lanes[h]] for h in range(nh)]

    def key_rows(kb):
        return pl.ds(pl.multiple_of(kb * tq, tq), tq)

    def logits(kb, h):
        return lax.dot_general(qs[h], k_ref[key_rows(kb), lanes[h]], (((1,), (1,)), ((), ())),
                               preferred_element_type=F32)

    def log_weights(z, masked):
        if masked:
            z = jnp.where(causal, z, NO_WEIGHT)
        log1p_term = jnp.log2(1.0 + jnp.exp2(-jnp.abs(z)))
        log_beta = jnp.minimum(z, 0.0) - log1p_term
        log_fail = log_beta - z
        tail = jnp.dot(log_fail.astype(BF16), suffix, preferred_element_type=F32)
        return log_beta + tail, jnp.sum(log_fail, axis=-1, keepdims=True)

    def weighted_values(kb, h, log_a, carry):
        a = jnp.exp2(log_a if carry is None else log_a + carry)
        return jnp.dot(a.astype(BF16), v_ref[key_rows(kb), lanes[h]], preferred_element_type=F32)

    def block(kb, h, carry):
        log_a, block_sum = log_weights(logits(kb, h), masked=False)
        return block_sum, weighted_values(kb, h, log_a, carry)

    prev = jnp.maximum(qi - 1, 0)
    no_prev = jnp.where(qi > 0, 0.0, NO_WEIGHT)
    units = [(h, kb, masked) for h in range(nh) for kb, masked in ((qi, True), (prev, False))]
    zs, lws, outs = {}, {}, {}
    for t in range(len(units) + 2 * STAGE_SKEW):
        if t < len(units):
            h, kb, _ = units[t]
            zs[t] = logits(kb, h)
        u = t - STAGE_SKEW
        if 0 <= u < len(units):
            lws[u] = log_weights(zs.pop(u), units[u][2])
        u = t - 2 * STAGE_SKEW
        if 0 <= u < len(units):
            h, kb, masked = units[u]
            carry = None if masked else lws[u - 1][1] + no_prev
            outs[u] = weighted_values(kb, h, lws[u][0], carry)
    state = [(lws[2 * h][1] + lws[2 * h + 1][1], outs[2 * h] + outs[2 * h + 1]) for h in range(nh)]

    def live(state):
        top = state[0][0]
        for h in range(1, nh):
            top = jnp.maximum(top, state[h][0])
        return jnp.max(top) >= ZERO_WEIGHT_LOG2

    def cond(loop_state):
        kb, alive, _ = loop_state
        return jnp.logical_and(kb >= 0, alive)

    def body(loop_state):
        kb, _, state = loop_state
        new_state = []
        for h in range(nh):
            carry, acc = state[h]
            block_sum, out = block(kb, h, carry)
            new_state.append((carry + block_sum, acc + out))
        new_state = tuple(new_state)
        return kb - 1, live(new_state), new_state

    state = tuple(state)
    _, _, state = lax.while_loop(cond, body, (qi - 2, live(state), state))
    for h in range(nh):
        o_ref[:, lanes[h]] = (state[h][1] * sg_ref[:, lanes[h]].astype(F32)).astype(BF16)


def _sb_attn(q, k, v, sg, *, batch, seq, tq, nh):
    tokens, width = q.shape
    heads = width // SB_HEAD_DIM
    nq = seq // tq
    grid = (batch, heads // nh, nq)
    q_spec = pl.BlockSpec((tq, nh * SB_HEAD_DIM), lambda b, h, i: (b * nq + i, h))
    kv_spec = pl.BlockSpec((seq, nh * SB_HEAD_DIM), lambda b, h, i: (b, h))
    return pl.pallas_call(
        functools.partial(_sb_attn_kernel, tq=tq, nh=nh),
        grid=grid,
        in_specs=[q_spec, kv_spec, kv_spec, q_spec],
        out_specs=q_spec,
        out_shape=jax.ShapeDtypeStruct((tokens, width), BF16),
        compiler_params=_params(("parallel", "parallel", "arbitrary")),
        name="sb_attention",
    )(q, k, v, sg)


def _sb_out_kernel(h_ref, o_ref, wo_ref, g_ref, out_ref, wo_bf16_ref):
    @pl.when(pl.program_id(0) == 0)
    def _():
        wo_bf16_ref[...] = wo_ref[...].astype(BF16)

    h2 = h_ref[...] + jnp.dot(o_ref[...], wo_bf16_ref[...], preferred_element_type=F32)
    out_ref[...] = _rms_normalize(h2, g_ref[...])


def _sb_out(h, og, w_out, final_g, *, tm):
    tokens, d_model = h.shape
    width = og.shape[1]
    return pl.pallas_call(
        _sb_out_kernel,
        grid=(tokens // tm,),
        in_specs=[
            pl.BlockSpec((tm, d_model), lambda i: (i, 0)),
            pl.BlockSpec((tm, width), lambda i: (i, 0)),
            pl.BlockSpec((width, d_model), lambda i: (0, 0), pipeline_mode=pl.Buffered(1)),
            pl.BlockSpec((1, d_model), lambda i: (0, 0)),
        ],
        out_specs=pl.BlockSpec((tm, d_model), lambda i: (i, 0)),
        out_shape=jax.ShapeDtypeStruct((tokens, d_model), F32),
        scratch_shapes=[pltpu.VMEM((width, d_model), BF16)],
        compiler_params=_params(("arbitrary",)),
        name="sb_out_proj_norm",
    )(h, og, w_out, final_g)


def kernel(x, norm_g, a_w_in, a_v_norm_g, a_w_s, a_b_s, a_w_out, b_w_in, b_w_out, final_g):
    batch, seq, d_model = x.shape
    assert norm_g.shape[0] == 2 and a_w_in.shape[0] == 1 and b_w_in.shape[0] == 1
    x2 = x.reshape(batch * seq, d_model)
    ug, v, rinv = _gmlp_in(x2, norm_g[0][None, :], a_w_in[0], tm=1024, tn=512)
    h1, hn1 = _gmlp_out(x2, v, ug, rinv, a_v_norm_g[0][None, :], a_w_s[0], a_b_s[0][:, :, None],
                        a_w_out[0].astype(BF16), norm_g[1][None, :], tm=256)
    q, k, v, sg = _sb_in(hn1, b_w_in[0], tm=1024, tn=256)
    og = _sb_attn(q, k, v, sg, batch=batch, seq=seq, tq=256, nh=8)
    out = _sb_out(h1, og, b_w_out[0], final_g[None, :], tm=512)
    return out.reshape(batch, seq, d_model)
```

```python
import functools

import jax
import jax.numpy as jnp
from jax import lax
from jax.experimental import pallas as pl
from jax.experimental.pallas import tpu as pltpu

EPS = 1e-6
CHUNK = 128
GMLP_GROUPS = 16
SB_HEAD_DIM = 128
GELU_C = 0.7978845608028654
LOG2_E = 1.4426950408889634
ZERO_WEIGHT_LOG2 = -160.0
NO_WEIGHT = -1e30
MATMUL_ROW_SPLITS = 4
STAGE_SKEW = 1

V7X_VMEM_LIMIT_BYTES = 60 * 1024 * 1024

BF16 = jnp.bfloat16
F32 = jnp.float32


def _gelu_tanh(x):
    return 0.5 * x * (1.0 + jnp.tanh(GELU_C * (x + 0.044715 * (x * x * x))))


def _silu(x):
    return 0.5 * x * (1.0 + jnp.tanh(0.5 * x))


def _rms_normalize(x_f32, gain_f32):
    ms = jnp.mean(x_f32 * x_f32, axis=-1, keepdims=True)
    return x_f32 * lax.rsqrt(ms + EPS) * gain_f32


def _row_splits(rows, parts):
    step = rows // parts
    return [slice(r * step, (r + 1) * step) for r in range(parts)]


def _dot_w(x_bf16, w_ref):
    return jnp.dot(x_bf16, w_ref[...].astype(BF16), preferred_element_type=F32)


def _params(semantics):
    return pltpu.CompilerParams(dimension_semantics=semantics,
                                vmem_limit_bytes=V7X_VMEM_LIMIT_BYTES)


def _gmlp_in_kernel(x_ref, g_ref, wu_ref, wv_ref, wg_ref, ug_ref, v_ref, rinv_ref,
                    xn_ref, ssq_ref, *, width):
    j = pl.program_id(1)

    @pl.when(j == 0)
    def _():
        xn_ref[...] = _rms_normalize(x_ref[...], g_ref[...]).astype(BF16)
        ssq_ref[...] = jnp.zeros_like(ssq_ref)

    wu, wv, wg = (w_ref[...].astype(BF16) for w_ref in (wu_ref, wv_ref, wg_ref))
    for rows in _row_splits(xn_ref.shape[0], MATMUL_ROW_SPLITS):
        xn = xn_ref[rows, :]
        v = _gelu_tanh(jnp.dot(xn, wv, preferred_element_type=F32))
        u = jnp.dot(xn, wu, preferred_element_type=F32)
        zg = jnp.dot(xn, wg, preferred_element_type=F32)
        v_ref[rows, :] = v.astype(BF16)
        ssq_ref[rows, :] += jnp.sum(v * v, axis=-1, keepdims=True)
        ug_ref[rows, :] = (_gelu_tanh(u) * _silu(zg)).astype(BF16)

    @pl.when(j == pl.num_programs(1) - 1)
    def _():
        rinv_ref[...] = lax.rsqrt(ssq_ref[...] * (1.0 / width) + EPS)


def _gmlp_in(x2, gain, w_in, *, tm, tn):
    tokens, d_model = x2.shape
    width = w_in.shape[1] // 3
    nj = width // tn
    grid = (tokens // tm, nj)
    w_spec = lambda off: pl.BlockSpec((d_model, tn), lambda i, j, off=off: (0, j + off))
    return pl.pallas_call(
        functools.partial(_gmlp_in_kernel, width=width),
        grid=grid,
        in_specs=[
            pl.BlockSpec((tm, d_model), lambda i, j: (i, 0)),
            pl.BlockSpec((1, d_model), lambda i, j: (0, 0)),
            w_spec(0), w_spec(nj), w_spec(2 * nj),
        ],
        out_specs=[
            pl.BlockSpec((tm, tn), lambda i, j: (i, j)),
            pl.BlockSpec((tm, tn), lambda i, j: (i, j)),
            pl.BlockSpec((tm, 1), lambda i, j: (i, 0)),
        ],
        out_shape=[
            jax.ShapeDtypeStruct((tokens, width), BF16),
            jax.ShapeDtypeStruct((tokens, width), BF16),
            jax.ShapeDtypeStruct((tokens, 1), F32),
        ],
        scratch_shapes=[pltpu.VMEM((tm, d_model), BF16), pltpu.VMEM((tm, 1), F32)],
        compiler_params=_params(("parallel", "arbitrary")),
        name="gmlp_in_proj",
    )(x2, gain, w_in, w_in, w_in)


def _gmlp_out_kernel(x_ref, v_ref, ug_ref, rinv_ref, vg_ref, ws_ref, bs_ref, wo_ref, ng_ref,
                     h_ref, hn_ref, y_ref, *, tm, gd, splits):
    row = lax.broadcasted_iota(jnp.int32, (CHUNK, CHUNK), 0)
    col = lax.broadcasted_iota(jnp.int32, (CHUNK, CHUNK), 1)
    ws = [jnp.where(row >= col, ws_ref[g], 0.0).astype(BF16) for g in range(GMLP_GROUPS)]

    def gate(part):
        for c in range(part.start // CHUNK, part.stop // CHUNK):
            rows = slice(c * CHUNK, (c + 1) * CHUNK)
            rinv = rinv_ref[rows, :]
            for g in range(GMLP_GROUPS):
                cols = slice(g * gd, (g + 1) * gd)
                vn = (v_ref[rows, cols].astype(F32) * rinv * vg_ref[:, cols]).astype(BF16)
                mixed = jnp.dot(ws[g], vn, preferred_element_type=F32) + bs_ref[:, g:g + 1]
                y_ref[rows, cols] = (ug_ref[rows, cols].astype(F32) * mixed).astype(BF16)

    def project(part):
        h = x_ref[part, :] + jnp.dot(y_ref[part, :], wo_ref[...], preferred_element_type=F32)
        h_ref[part, :] = h
        hn_ref[part, :] = _rms_normalize(h, ng_ref[...]).astype(BF16)

    for part in _row_splits(tm, splits):
        gate(part)
        project(part)


def _gmlp_out(x2, v, ug, rinv, v_gain, w_s, b_s_t, w_out_bf16, next_gain, *, tm, splits):
    tokens, d_model = x2.shape
    width = v.shape[1]
    gd = width // GMLP_GROUPS
    row_tile = lambda cols: pl.BlockSpec((tm, cols), lambda i: (i, 0))
    whole = lambda shape: pl.BlockSpec(shape, lambda i: (0,) * len(shape), pipeline_mode=pl.Buffered(1))
    return pl.pallas_call(
        functools.partial(_gmlp_out_kernel, tm=tm, gd=gd, splits=splits),
        grid=(tokens // tm,),
        in_specs=[
            row_tile(d_model), row_tile(width), row_tile(width), row_tile(1),
            whole((1, width)), whole((GMLP_GROUPS, CHUNK, CHUNK)), whole((CHUNK, GMLP_GROUPS)),
            pl.BlockSpec((width, d_model), lambda i: (0, 0), pipeline_mode=pl.Buffered(1)),
            whole((1, d_model)),
        ],
        out_specs=[row_tile(d_model)] * 2,
        out_shape=[jax.ShapeDtypeStruct((tokens, d_model), F32),
                   jax.ShapeDtypeStruct((tokens, d_model), BF16)],
        scratch_shapes=[pltpu.VMEM((tm, width), BF16)],
        compiler_params=_params(("parallel",)),
        name="gmlp_mix_out_proj",
    )(x2, v, ug, rinv, v_gain, w_s, b_s_t, w_out_bf16, next_gain)


def _sb_in_kernel(hn_ref, wq_ref, wk_ref, wv_ref, wg_ref, q_ref, k_ref, v_ref, sg_ref, w_bf16_ref):
    @pl.when(pl.program_id(1) == 0)
    def _():
        for n, w_ref in enumerate((wq_ref, wk_ref, wv_ref, wg_ref)):
            w_bf16_ref[n] = w_ref[...].astype(BF16)

    hn = hn_ref[...]
    dot = lambda n: jnp.dot(hn, w_bf16_ref[n], preferred_element_type=F32)
    q_ref[...] = (dot(0) * (SB_HEAD_DIM ** -0.5 * LOG2_E)).astype(BF16)
    k_ref[...] = dot(1).astype(BF16)
    v_ref[...] = dot(2).astype(BF16)
    sg_ref[...] = _silu(dot(3)).astype(BF16)


def _sb_in(hn, w_in, *, tm, tn):
    tokens, d_model = hn.shape
    width = w_in.shape[1] // 4
    nj = width // tn
    grid = (nj, tokens // tm)
    w_spec = lambda off: pl.BlockSpec((d_model, tn), lambda j, i, off=off: (0, j + off))
    out_spec = pl.BlockSpec((tm, tn), lambda j, i: (i, j))
    out_shape = jax.ShapeDtypeStruct((tokens, width), BF16)
    return pl.pallas_call(
        _sb_in_kernel,
        grid=grid,
        in_specs=[
            pl.BlockSpec((tm, d_model), lambda j, i: (i, 0)),
            w_spec(0), w_spec(nj), w_spec(2 * nj), w_spec(3 * nj),
        ],
        out_specs=[out_spec] * 4,
        out_shape=[out_shape] * 4,
        scratch_shapes=[pltpu.VMEM((4, d_model, tn), BF16)],
        compiler_params=_params(("parallel", "arbitrary")),
        name="sb_in_proj",
    )(hn, w_in, w_in, w_in, w_in)


def _sb_attn_kernel(q_ref, k_ref, v_ref, sg_ref, o_ref, *, tq, nh):
    qi = pl.program_id(2)
    row = lax.broadcasted_iota(jnp.int32, (tq, tq), 0)
    col = lax.broadcasted_iota(jnp.int32, (tq, tq), 1)
    suffix = (row > col).astype(BF16)
    causal = col < row
    lanes = [slice(h * SB_HEAD_DIM, (h + 1) * SB_HEAD_DIM) for h in range(nh)]
    qs = [q_ref[:, lanes[h]] for h in range(nh)]

    def key_rows(kb):
        return pl.ds(pl.multiple_of(kb * tq, tq), tq)

    def logits(kb, h):
        return lax.dot_general(qs[h], k_ref[key_rows(kb), lanes[h]], (((1,), (1,)), ((), ())),
                               preferred_element_type=F32)

    def log_weights(z, masked):
        if masked:
            z = jnp.where(causal, z, NO_WEIGHT)
        log1p_term = jnp.log2(1.0 + jnp.exp2(-jnp.abs(z)))
        log_beta = jnp.minimum(z, 0.0) - log1p_term
        log_fail = log_beta - z
        tail = jnp.dot(log_fail.astype(BF16), suffix, preferred_element_type=F32)
        return log_beta + tail, jnp.sum(log_fail, axis=-1, keepdims=True)

    def weighted_values(kb, h, log_a, carry):
        a = jnp.exp2(log_a if carry is None else log_a + carry)
        return jnp.dot(a.astype(BF16), v_ref[key_rows(kb), lanes[h]], preferred_element_type=F32)

    def block(kb, h, carry):
        log_a, block_sum = log_weights(logits(kb, h), masked=False)
        return block_sum, weighted_values(kb, h, log_a, carry)

    prev = jnp.maximum(qi - 1, 0)
    no_prev = jnp.where(qi > 0, 0.0, NO_WEIGHT)
    units = [(h, kb, masked) for h in range(nh) for kb, masked in ((qi, True), (prev, False))]
    zs, lws, outs = {}, {}, {}
    for t in range(len(units) + 2 * STAGE_SKEW):
        if t < len(units):
            h, kb, _ = units[t]
            zs[t] = logits(kb, h)
        u = t - STAGE_SKEW
        if 0 <= u < len(units):
            lws[u] = log_weights(zs.pop(u), units[u][2])
        u = t - 2 * STAGE_SKEW
        if 0 <= u < len(units):
            h, kb, masked = units[u]
            carry = None if masked else lws[u - 1][1] + no_prev
            outs[u] = weighted_values(kb, h, lws[u][0], carry)
    state = [(lws[2 * h][1] + lws[2 * h + 1][1], outs[2 * h] + outs[2 * h + 1]) for h in range(nh)]

    def live(state):
        top = state[0][0]
        for h in range(1, nh):
            top = jnp.maximum(top, state[h][0])
        return jnp.max(top) >= ZERO_WEIGHT_LOG2

    def cond(loop_state):
        kb, alive, _ = loop_state
        return jnp.logical_and(kb >= 0, alive)

    def body(loop_state):
        kb, _, state = loop_state
        new_state = []
        for h in range(nh):
            carry, acc = state[h]
            block_sum, out = block(kb, h, carry)
            new_state.append((carry + block_sum, acc + out))
        new_state = tuple(new_state)
        return kb - 1, live(new_state), new_state

    state = tuple(state)
    _, _, state = lax.while_loop(cond, body, (qi - 2, live(state), state))
    for h in range(nh):
        o_ref[:, lanes[h]] = (state[h][1] * sg_ref[:, lanes[h]].astype(F32)).astype(BF16)


def _sb_attn(q, k, v, sg, *, batch, seq, tq, nh):
    tokens, width = q.shape
    heads = width // SB_HEAD_DIM
    nq = seq // tq
    grid = (batch, heads // nh, nq)
    q_spec = pl.BlockSpec((tq, nh * SB_HEAD_DIM), lambda b, h, i: (b * nq + i, h))
    kv_spec = pl.BlockSpec((seq, nh * SB_HEAD_DIM), lambda b, h, i: (b, h))
    return pl.pallas_call(
        functools.partial(_sb_attn_kernel, tq=tq, nh=nh),
        grid=grid,
        in_specs=[q_spec, kv_spec, kv_spec, q_spec],
        out_specs=q_spec,
        out_shape=jax.ShapeDtypeStruct((tokens, width), BF16),
        compiler_params=_params(("parallel", "parallel", "arbitrary")),
        name="sb_attention",
    )(q, k, v, sg)


def _sb_out_kernel(h_ref, o_ref, wo_ref, g_ref, out_ref, wo_bf16_ref):
    @pl.when(pl.program_id(0) == 0)
    def _():
        wo_bf16_ref[...] = wo_ref[...].astype(BF16)

    h2 = h_ref[...] + jnp.dot(o_ref[...], wo_bf16_ref[...], preferred_element_type=F32)
    out_ref[...] = _rms_normalize(h2, g_ref[...])


def _sb_out(h, og, w_out, final_g, *, tm):
    tokens, d_model = h.shape
    width = og.shape[1]
    return pl.pallas_call(
        _sb_out_kernel,
        grid=(tokens // tm,),
        in_specs=[
            pl.BlockSpec((tm, d_model), lambda i: (i, 0)),
            pl.BlockSpec((tm, width), lambda i: (i, 0)),
            pl.BlockSpec((width, d_model), lambda i: (0, 0), pipeline_mode=pl.Buffered(1)),
            pl.BlockSpec((1, d_model), lambda i: (0, 0)),
        ],
        out_specs=pl.BlockSpec((tm, d_model), lambda i: (i, 0)),
        out_shape=jax.ShapeDtypeStruct((tokens, d_model), F32),
        scratch_shapes=[pltpu.VMEM((width, d_model), BF16)],
        compiler_params=_params(("arbitrary",)),
        name="sb_out_proj_norm",
    )(h, og, w_out, final_g)


def kernel(x, norm_g, a_w_in, a_v_norm_g, a_w_s, a_b_s, a_w_out, b_w_in, b_w_out, final_g):
    batch, seq, d_model = x.shape
    assert norm_g.shape[0] == 2 and a_w_in.shape[0] == 1 and b_w_in.shape[0] == 1
    x2 = x.reshape(batch * seq, d_model)
    ug, v, rinv = _gmlp_in(x2, norm_g[0][None, :], a_w_in[0], tm=1024, tn=512)
    h1, hn1 = _gmlp_out(x2, v, ug, rinv, a_v_norm_g[0][None, :], a_w_s[0], a_b_s[0].T,
                        a_w_out[0].astype(BF16), norm_g[1][None, :], tm=512, splits=2)
    q, k, v, sg = _sb_in(hn1, b_w_in[0], tm=1024, tn=256)
    og = _sb_attn(q, k, v, sg, batch=batch, seq=seq, tq=256, nh=8)
    out = _sb_out(h1, og, b_w_out[0], final_g[None, :], tm=512)
    return out.reshape(batch, seq, d_model)
```

```python
import functools

import jax
import jax.numpy as jnp
from jax import lax
from jax.experimental import pallas as pl
from jax.experimental.pallas import tpu as pltpu

EPS = 1e-6
CHUNK = 128
GMLP_GROUPS = 16
SB_HEAD_DIM = 128
GELU_C = 0.7978845608028654
LOG2_E = 1.4426950408889634
ZERO_WEIGHT_LOG2 = -160.0
NO_WEIGHT = -1e30
MATMUL_ROW_SPLITS = 4
STAGE_SKEW = 1

V7X_VMEM_LIMIT_BYTES = 60 * 1024 * 1024

BF16 = jnp.bfloat16
F32 = jnp.float32


def _gelu_tanh(x):
    return 0.5 * x * (1.0 + jnp.tanh(GELU_C * (x + 0.044715 * (x * x * x))))


def _silu(x):
    return 0.5 * x * (1.0 + jnp.tanh(0.5 * x))


def _rms_normalize(x_f32, gain_f32):
    ms = jnp.mean(x_f32 * x_f32, axis=-1, keepdims=True)
    return x_f32 * lax.rsqrt(ms + EPS) * gain_f32


def _row_splits(rows, parts):
    step = rows // parts
    return [slice(r * step, (r + 1) * step) for r in range(parts)]


def _params(semantics):
    return pltpu.CompilerParams(dimension_semantics=semantics,
                                vmem_limit_bytes=V7X_VMEM_LIMIT_BYTES)


def _gmlp_in_kernel(x_ref, g_ref, wu_ref, wv_ref, wg_ref, wo_ref, ug_ref, v_ref, rinv_ref, wo_bf16_ref,
                    xn_ref, ssq_ref, *, width):
    j = pl.program_id(1)
    wo_bf16_ref[...] = wo_ref[...].astype(BF16)

    @pl.when(j == 0)
    def _():
        xn_ref[...] = _rms_normalize(x_ref[...], g_ref[...]).astype(BF16)
        ssq_ref[...] = jnp.zeros_like(ssq_ref)

    wu, wv, wg = (w_ref[...].astype(BF16) for w_ref in (wu_ref, wv_ref, wg_ref))
    for rows in _row_splits(xn_ref.shape[0], MATMUL_ROW_SPLITS):
        xn = xn_ref[rows, :]
        v = _gelu_tanh(jnp.dot(xn, wv, preferred_element_type=F32))
        u = jnp.dot(xn, wu, preferred_element_type=F32)
        zg = jnp.dot(xn, wg, preferred_element_type=F32)
        v_ref[rows, :] = v.astype(BF16)
        ssq_ref[rows, :] += jnp.sum(v * v, axis=-1, keepdims=True)
        ug_ref[rows, :] = (_gelu_tanh(u) * _silu(zg)).astype(BF16)

    @pl.when(j == pl.num_programs(1) - 1)
    def _():
        rinv_ref[...] = lax.rsqrt(ssq_ref[...] * (1.0 / width) + EPS)


def _gmlp_in(x2, gain, w_in, w_out, *, tm, tn):
    tokens, d_model = x2.shape
    width = w_in.shape[1] // 3
    nj = width // tn
    grid = (tokens // tm, nj)
    w_spec = lambda off: pl.BlockSpec((d_model, tn), lambda i, j, off=off: (0, j + off))
    slab = w_out.shape[0] // (grid[0] * grid[1])
    slab_spec = pl.BlockSpec((slab, d_model), lambda i, j: (i * nj + j, 0))
    return pl.pallas_call(
        functools.partial(_gmlp_in_kernel, width=width),
        grid=grid,
        in_specs=[
            pl.BlockSpec((tm, d_model), lambda i, j: (i, 0)),
            pl.BlockSpec((1, d_model), lambda i, j: (0, 0)),
            w_spec(0), w_spec(nj), w_spec(2 * nj),
            slab_spec,
        ],
        out_specs=[
            pl.BlockSpec((tm, tn), lambda i, j: (i, j)),
            pl.BlockSpec((tm, tn), lambda i, j: (i, j)),
            pl.BlockSpec((tm, 1), lambda i, j: (i, 0)),
            slab_spec,
        ],
        out_shape=[
            jax.ShapeDtypeStruct((tokens, width), BF16),
            jax.ShapeDtypeStruct((tokens, width), BF16),
            jax.ShapeDtypeStruct((tokens, 1), F32),
            jax.ShapeDtypeStruct(w_out.shape, BF16),
        ],
        scratch_shapes=[pltpu.VMEM((tm, d_model), BF16), pltpu.VMEM((tm, 1), F32)],
        compiler_params=_params(("parallel", "arbitrary")),
        name="gmlp_in_proj",
    )(x2, gain, w_in, w_in, w_in, w_out)


def _gmlp_out_kernel(x_ref, v_ref, ug_ref, rinv_ref, vg_ref, ws_ref, bs_ref, wo_ref, ng_ref,
                     h_ref, hn_ref, y_ref, *, tm, gd, splits):
    row = lax.broadcasted_iota(jnp.int32, (CHUNK, CHUNK), 0)
    col = lax.broadcasted_iota(jnp.int32, (CHUNK, CHUNK), 1)
    ws = [jnp.where(row >= col, ws_ref[g], 0.0).astype(BF16) for g in range(GMLP_GROUPS)]

    def gate(part):
        for c in range(part.start // CHUNK, part.stop // CHUNK):
            rows = slice(c * CHUNK, (c + 1) * CHUNK)
            rinv = rinv_ref[rows, :]
            for g in range(GMLP_GROUPS):
                cols = slice(g * gd, (g + 1) * gd)
                vn = (v_ref[rows, cols].astype(F32) * rinv * vg_ref[:, cols]).astype(BF16)
                mixed = jnp.dot(ws[g], vn, preferred_element_type=F32) + bs_ref[:, g:g + 1]
                y_ref[rows, cols] = (ug_ref[rows, cols].astype(F32) * mixed).astype(BF16)

    def project(part):
        h = x_ref[part, :] + jnp.dot(y_ref[part, :], wo_ref[...], preferred_element_type=F32)
        h_ref[part, :] = h
        hn_ref[part, :] = _rms_normalize(h, ng_ref[...]).astype(BF16)

    for part in _row_splits(tm, splits):
        gate(part)
        project(part)


def _gmlp_out(x2, v, ug, rinv, v_gain, w_s, b_s_t, w_out_bf16, next_gain, *, tm, splits):
    tokens, d_model = x2.shape
    width = v.shape[1]
    gd = width // GMLP_GROUPS
    row_tile = lambda cols: pl.BlockSpec((tm, cols), lambda i: (i, 0))
    whole = lambda shape: pl.BlockSpec(shape, lambda i: (0,) * len(shape), pipeline_mode=pl.Buffered(1))
    return pl.pallas_call(
        functools.partial(_gmlp_out_kernel, tm=tm, gd=gd, splits=splits),
        grid=(tokens // tm,),
        in_specs=[
            row_tile(d_model), row_tile(width), row_tile(width), row_tile(1),
            whole((1, width)), whole((GMLP_GROUPS, CHUNK, CHUNK)), whole((CHUNK, GMLP_GROUPS)),
            whole((width, d_model)), whole((1, d_model)),
        ],
        out_specs=[row_tile(d_model)] * 2,
        out_shape=[jax.ShapeDtypeStruct((tokens, d_model), F32),
                   jax.ShapeDtypeStruct((tokens, d_model), BF16)],
        scratch_shapes=[pltpu.VMEM((tm, width), BF16)],
        compiler_params=_params(("parallel",)),
        name="gmlp_mix_out_proj",
    )(x2, v, ug, rinv, v_gain, w_s, b_s_t, w_out_bf16, next_gain)


def _sb_in_kernel(hn_ref, wq_ref, wk_ref, wv_ref, wg_ref, q_ref, k_ref, v_ref, sg_ref, w_bf16_ref, *, splits):
    @pl.when(pl.program_id(1) == 0)
    def _():
        for n, w_ref in enumerate((wq_ref, wk_ref, wv_ref, wg_ref)):
            w_bf16_ref[n] = w_ref[...].astype(BF16)

    for rows in _row_splits(hn_ref.shape[0], splits):
        hn = hn_ref[rows, :]
        dot = lambda n: jnp.dot(hn, w_bf16_ref[n], preferred_element_type=F32)
        q_ref[rows, :] = (dot(0) * (SB_HEAD_DIM ** -0.5 * LOG2_E)).astype(BF16)
        k_ref[rows, :] = dot(1).astype(BF16)
        v_ref[rows, :] = dot(2).astype(BF16)
        sg_ref[rows, :] = _silu(dot(3)).astype(BF16)


def _sb_in(hn, w_in, *, tm, tn, splits):
    tokens, d_model = hn.shape
    width = w_in.shape[1] // 4
    nj = width // tn
    grid = (nj, tokens // tm)
    w_spec = lambda off: pl.BlockSpec((d_model, tn), lambda j, i, off=off: (0, j + off))
    out_spec = pl.BlockSpec((tm, tn), lambda j, i: (i, j))
    out_shape = jax.ShapeDtypeStruct((tokens, width), BF16)
    return pl.pallas_call(
        functools.partial(_sb_in_kernel, splits=splits),
        grid=grid,
        in_specs=[
            pl.BlockSpec((tm, d_model), lambda j, i: (i, 0)),
            w_spec(0), w_spec(nj), w_spec(2 * nj), w_spec(3 * nj),
        ],
        out_specs=[out_spec] * 4,
        out_shape=[out_shape] * 4,
        scratch_shapes=[pltpu.VMEM((4, d_model, tn), BF16)],
        compiler_params=_params(("parallel", "arbitrary")),
        name="sb_in_proj",
    )(hn, w_in, w_in, w_in, w_in)


def _sb_attn_kernel(q_ref, k_ref, v_ref, sg_ref, o_ref, *, tq, nh):
    qi = pl.program_id(2)
    row = lax.broadcasted_iota(jnp.int32, (tq, tq), 0)
    col = lax.broadcasted_iota(jnp.int32, (tq, tq), 1)
    suffix = (row > col).astype(BF16)
    causal = col < row
    lanes = [slice(h * SB_HEAD_DIM, (h + 1) * SB_HEAD_DIM) for h in range(nh)]
    qs = [q_ref[:, lanes[h]] for h in range(nh)]

    def key_rows(kb):
        return pl.ds(pl.multiple_of(kb * tq, tq), tq)

    def logits(kb, h):
        return lax.dot_general(qs[h], k_ref[key_rows(kb), lanes[h]], (((1,), (1,)), ((), ())),
                               preferred_element_type=F32)

    def log_weights(z, masked):
        if masked:
            z = jnp.where(causal, z, NO_WEIGHT)
        log1p_term = jnp.log2(1.0 + jnp.exp2(-jnp.abs(z)))
        log_beta = jnp.minimum(z, 0.0) - log1p_term
        log_fail = log_beta - z
        tail = jnp.dot(log_fail.astype(BF16), suffix, preferred_element_type=F32)
        return log_beta + tail, jnp.sum(log_fail, axis=-1, keepdims=True)

    def weighted_values(kb, h, log_a, carry):
        a = jnp.exp2(log_a if carry is None else log_a + carry)
        return jnp.dot(a.astype(BF16), v_ref[key_rows(kb), lanes[h]], preferred_element_type=F32)

    def block(kb, h, carry):
        log_a, block_sum = log_weights(logits(kb, h), masked=False)
        return block_sum, weighted_values(kb, h, log_a, carry)

    prev = jnp.maximum(qi - 1, 0)
    no_prev = jnp.where(qi > 0, 0.0, NO_WEIGHT)
    units = [(h, kb, masked) for h in range(nh) for kb, masked in ((qi, True), (prev, False))]
    zs, lws, outs = {}, {}, {}
    for t in range(len(units) + 2 * STAGE_SKEW):
        if t < len(units):
            h, kb, _ = units[t]
            zs[t] = logits(kb, h)
        u = t - STAGE_SKEW
        if 0 <= u < len(units):
            lws[u] = log_weights(zs.pop(u), units[u][2])
        u = t - 2 * STAGE_SKEW
        if 0 <= u < len(units):
            h, kb, masked = units[u]
            carry = None if masked else lws[u - 1][1] + no_prev
            outs[u] = weighted_values(kb, h, lws[u][0], carry)
    state = [(lws[2 * h][1] + lws[2 * h + 1][1], outs[2 * h] + outs[2 * h + 1]) for h in range(nh)]

    def live(state):
        top = state[0][0]
        for h in range(1, nh):
            top = jnp.maximum(top, state[h][0])
        return jnp.max(top) >= ZERO_WEIGHT_LOG2

    def cond(loop_state):
        kb, alive, _ = loop_state
        return jnp.logical_and(kb >= 0, alive)

    def body(loop_state):
        kb, _, state = loop_state
        new_state = []
        for h in range(nh):
            carry, acc = state[h]
            block_sum, out = block(kb, h, carry)
            new_state.append((carry + block_sum, acc + out))
        new_state = tuple(new_state)
        return kb - 1, live(new_state), new_state

    state = tuple(state)
    _, _, state = lax.while_loop(cond, body, (qi - 2, live(state), state))
    for h in range(nh):
        o_ref[:, lanes[h]] = (state[h][1] * sg_ref[:, lanes[h]].astype(F32)).astype(BF16)


def _sb_attn(q, k, v, sg, *, batch, seq, tq, nh):
    tokens, width = q.shape
    heads = width // SB_HEAD_DIM
    nq = seq // tq
    grid = (batch, heads // nh, nq)
    q_spec = pl.BlockSpec((tq, nh * SB_HEAD_DIM), lambda b, h, i: (b * nq + i, h))
    kv_spec = pl.BlockSpec((seq, nh * SB_HEAD_DIM), lambda b, h, i: (b, h))
    return pl.pallas_call(
        functools.partial(_sb_attn_kernel, tq=tq, nh=nh),
        grid=grid,
        in_specs=[q_spec, kv_spec, kv_spec, q_spec],
        out_specs=q_spec,
        out_shape=jax.ShapeDtypeStruct((tokens, width), BF16),
        compiler_params=_params(("parallel", "parallel", "arbitrary")),
        name="sb_attention",
    )(q, k, v, sg)


def _sb_out_kernel(h_ref, o_ref, wo_ref, g_ref, out_ref, wo_bf16_ref, *, splits):
    @pl.when(pl.program_id(0) == 0)
    def _():
        wo_bf16_ref[...] = wo_ref[...].astype(BF16)

    for rows in _row_splits(h_ref.shape[0], splits):
        h2 = h_ref[rows, :] + jnp.dot(o_ref[rows, :], wo_bf16_ref[...], preferred_element_type=F32)
        out_ref[rows, :] = _rms_normalize(h2, g_ref[...])


def _sb_out(h, og, w_out, final_g, *, tm, splits):
    tokens, d_model = h.shape
    width = og.shape[1]
    return pl.pallas_call(
        functools.partial(_sb_out_kernel, splits=splits),
        grid=(tokens // tm,),
        in_specs=[
            pl.BlockSpec((tm, d_model), lambda i: (i, 0)),
            pl.BlockSpec((tm, width), lambda i: (i, 0)),
            pl.BlockSpec((width, d_model), lambda i: (0, 0), pipeline_mode=pl.Buffered(1)),
            pl.BlockSpec((1, d_model), lambda i: (0, 0)),
        ],
        out_specs=pl.BlockSpec((tm, d_model), lambda i: (i, 0)),
        out_shape=jax.ShapeDtypeStruct((tokens, d_model), F32),
        scratch_shapes=[pltpu.VMEM((width, d_model), BF16)],
        compiler_params=_params(("arbitrary",)),
        name="sb_out_proj_norm",
    )(h, og, w_out, final_g)


def kernel(x, norm_g, a_w_in, a_v_norm_g, a_w_s, a_b_s, a_w_out, b_w_in, b_w_out, final_g):
    batch, seq, d_model = x.shape
    assert norm_g.shape[0] == 2 and a_w_in.shape[0] == 1 and b_w_in.shape[0] == 1
    x2 = x.reshape(batch * seq, d_model)
    ug, v, rinv, a_w_out_bf16 = _gmlp_in(x2, norm_g[0][None, :], a_w_in[0], a_w_out[0], tm=1024, tn=512)
    h1, hn1 = _gmlp_out(x2, v, ug, rinv, a_v_norm_g[0][None, :], a_w_s[0], a_b_s[0].T,
                        a_w_out_bf16, norm_g[1][None, :], tm=512, splits=2)
    q, k, v, sg = _sb_in(hn1, b_w_in[0], tm=2048, tn=256, splits=2)
    og = _sb_attn(q, k, v, sg, batch=batch, seq=seq, tq=256, nh=8)
    out = _sb_out(h1, og, b_w_out[0], final_g[None, :], tm=512, splits=2)
    return out.reshape(batch, seq, d_model)
```

```python
import functools

import jax
import jax.numpy as jnp
from jax import lax
from jax.experimental import pallas as pl
from jax.experimental.pallas import tpu as pltpu

EPS = 1e-6
CHUNK = 128
GMLP_GROUPS = 16
SB_HEAD_DIM = 128
GELU_C = 0.7978845608028654
LOG2_E = 1.4426950408889634
ZERO_WEIGHT_LOG2 = -160.0
NO_WEIGHT = -1e30
MATMUL_ROW_SPLITS = 8
STAGE_SKEW = 1

V7X_VMEM_LIMIT_BYTES = 60 * 1024 * 1024

BF16 = jnp.bfloat16
F32 = jnp.float32


def _gelu_tanh(x):
    return 0.5 * x * (1.0 + jnp.tanh(GELU_C * (x + 0.044715 * (x * x * x))))


def _silu(x):
    return 0.5 * x * (1.0 + jnp.tanh(0.5 * x))


def _rms_normalize(x_f32, gain_f32):
    ms = jnp.mean(x_f32 * x_f32, axis=-1, keepdims=True)
    return x_f32 * lax.rsqrt(ms + EPS) * gain_f32


def _row_splits(rows, parts):
    step = rows // parts
    return [slice(r * step, (r + 1) * step) for r in range(parts)]


def _params(semantics):
    return pltpu.CompilerParams(dimension_semantics=semantics,
                                vmem_limit_bytes=V7X_VMEM_LIMIT_BYTES)


def _gmlp_in_kernel(x_ref, g_ref, wu_ref, wv_ref, wg_ref, wo_ref, ug_ref, v_ref, rinv_ref, wo_bf16_ref,
                    xn_ref, ssq_ref, *, width):
    j = pl.program_id(1)
    wo_bf16_ref[...] = wo_ref[...].astype(BF16)

    @pl.when(j == 0)
    def _():
        xn_ref[...] = _rms_normalize(x_ref[...], g_ref[...]).astype(BF16)
        ssq_ref[...] = jnp.zeros_like(ssq_ref)

    wu, wv, wg = (w_ref[...].astype(BF16) for w_ref in (wu_ref, wv_ref, wg_ref))
    for rows in _row_splits(xn_ref.shape[0], MATMUL_ROW_SPLITS):
        xn = xn_ref[rows, :]
        v = _gelu_tanh(jnp.dot(xn, wv, preferred_element_type=F32))
        u = jnp.dot(xn, wu, preferred_element_type=F32)
        zg = jnp.dot(xn, wg, preferred_element_type=F32)
        v_ref[rows, :] = v.astype(BF16)
        ssq_ref[rows, :] += jnp.sum(v * v, axis=-1, keepdims=True)
        ug_ref[rows, :] = (_gelu_tanh(u) * _silu(zg)).astype(BF16)

    @pl.when(j == pl.num_programs(1) - 1)
    def _():
        rinv_ref[...] = lax.rsqrt(ssq_ref[...] * (1.0 / width) + EPS)


def _gmlp_in(x2, gain, w_in, w_out, *, tm, tn):
    tokens, d_model = x2.shape
    width = w_in.shape[1] // 3
    nj = width // tn
    grid = (tokens // tm, nj)
    w_spec = lambda off: pl.BlockSpec((d_model, tn), lambda i, j, off=off: (0, j + off))
    slab = w_out.shape[0] // (grid[0] * grid[1])
    slab_spec = pl.BlockSpec((slab, d_model), lambda i, j: (i * nj + j, 0))
    return pl.pallas_call(
        functools.partial(_gmlp_in_kernel, width=width),
        grid=grid,
        in_specs=[
            pl.BlockSpec((tm, d_model), lambda i, j: (i, 0)),
            pl.BlockSpec((1, d_model), lambda i, j: (0, 0)),
            w_spec(0), w_spec(nj), w_spec(2 * nj),
            slab_spec,
        ],
        out_specs=[
            pl.BlockSpec((tm, tn), lambda i, j: (i, j)),
            pl.BlockSpec((tm, tn), lambda i, j: (i, j)),
            pl.BlockSpec((tm, 1), lambda i, j: (i, 0)),
            slab_spec,
        ],
        out_shape=[
            jax.ShapeDtypeStruct((tokens, width), BF16),
            jax.ShapeDtypeStruct((tokens, width), BF16),
            jax.ShapeDtypeStruct((tokens, 1), F32),
            jax.ShapeDtypeStruct(w_out.shape, BF16),
        ],
        scratch_shapes=[pltpu.VMEM((tm, d_model), BF16), pltpu.VMEM((tm, 1), F32)],
        compiler_params=_params(("parallel", "arbitrary")),
        name="gmlp_in_proj",
    )(x2, gain, w_in, w_in, w_in, w_out)


def _gmlp_out_kernel(x_ref, v_ref, ug_ref, rinv_ref, vg_ref, ws_ref, bs_ref, wo_ref, ng_ref,
                     h_ref, hn_ref, y_ref, *, tm, gd, splits):
    row = lax.broadcasted_iota(jnp.int32, (CHUNK, CHUNK), 0)
    col = lax.broadcasted_iota(jnp.int32, (CHUNK, CHUNK), 1)
    ws = [jnp.where(row >= col, ws_ref[g], 0.0).astype(BF16) for g in range(GMLP_GROUPS)]

    def gate(part):
        for c in range(part.start // CHUNK, part.stop // CHUNK):
            rows = slice(c * CHUNK, (c + 1) * CHUNK)
            rinv = rinv_ref[rows, :]
            for g in range(GMLP_GROUPS):
                cols = slice(g * gd, (g + 1) * gd)
                vn = (v_ref[rows, cols].astype(F32) * rinv * vg_ref[:, cols]).astype(BF16)
                mixed = jnp.dot(ws[g], vn, preferred_element_type=F32) + bs_ref[:, g:g + 1]
                y_ref[rows, cols] = (ug_ref[rows, cols].astype(F32) * mixed).astype(BF16)

    def project(part):
        h = x_ref[part, :] + jnp.dot(y_ref[part, :], wo_ref[...], preferred_element_type=F32)
        h_ref[part, :] = h
        hn_ref[part, :] = _rms_normalize(h, ng_ref[...]).astype(BF16)

    for part in _row_splits(tm, splits):
        gate(part)
        project(part)


def _gmlp_out(x2, v, ug, rinv, v_gain, w_s, b_s_t, w_out_bf16, next_gain, *, tm, splits):
    tokens, d_model = x2.shape
    width = v.shape[1]
    gd = width // GMLP_GROUPS
    row_tile = lambda cols: pl.BlockSpec((tm, cols), lambda i: (i, 0))
    whole = lambda shape: pl.BlockSpec(shape, lambda i: (0,) * len(shape), pipeline_mode=pl.Buffered(1))
    return pl.pallas_call(
        functools.partial(_gmlp_out_kernel, tm=tm, gd=gd, splits=splits),
        grid=(tokens // tm,),
        in_specs=[
            row_tile(d_model), row_tile(width), row_tile(width), row_tile(1),
            whole((1, width)), whole((GMLP_GROUPS, CHUNK, CHUNK)), whole((CHUNK, GMLP_GROUPS)),
            whole((width, d_model)), whole((1, d_model)),
        ],
        out_specs=[row_tile(d_model)] * 2,
        out_shape=[jax.ShapeDtypeStruct((tokens, d_model), F32),
                   jax.ShapeDtypeStruct((tokens, d_model), BF16)],
        scratch_shapes=[pltpu.VMEM((tm, width), BF16)],
        compiler_params=_params(("parallel",)),
        name="gmlp_mix_out_proj",
    )(x2, v, ug, rinv, v_gain, w_s, b_s_t, w_out_bf16, next_gain)


def _sb_in_kernel(hn_ref, wq_ref, wk_ref, wv_ref, wg_ref, wo_ref, q_ref, k_ref, v_ref, sg_ref, wo_bf16_ref,
                  w_bf16_ref, *, splits):
    @pl.when(pl.program_id(1) == 0)
    def _():
        for n, w_ref in enumerate((wq_ref, wk_ref, wv_ref, wg_ref)):
            w_bf16_ref[n] = w_ref[...].astype(BF16)

    wo_bf16_ref[...] = wo_ref[...].astype(BF16)

    for rows in _row_splits(hn_ref.shape[0], splits):
        hn = hn_ref[rows, :]
        dot = lambda n: jnp.dot(hn, w_bf16_ref[n], preferred_element_type=F32)
        q_ref[rows, :] = (dot(0) * (SB_HEAD_DIM ** -0.5 * LOG2_E)).astype(BF16)
        k_ref[rows, :] = dot(1).astype(BF16)
        v_ref[rows, :] = dot(2).astype(BF16)
        sg_ref[rows, :] = _silu(dot(3)).astype(BF16)


def _sb_in(hn, w_in, w_out, *, tm, tn, splits):
    tokens, d_model = hn.shape
    width = w_in.shape[1] // 4
    nj = width // tn
    ni = tokens // tm
    w_spec = lambda off: pl.BlockSpec((d_model, tn), lambda j, i, off=off: (0, j + off))
    out_spec = pl.BlockSpec((tm, tn), lambda j, i: (i, j))
    out_shape = jax.ShapeDtypeStruct((tokens, width), BF16)
    slab = w_out.shape[0] // (nj * ni)
    slab_spec = pl.BlockSpec((slab, w_out.shape[1]), lambda j, i: (j * ni + i, 0))
    return pl.pallas_call(
        functools.partial(_sb_in_kernel, splits=splits),
        grid=(nj, ni),
        in_specs=[
            pl.BlockSpec((tm, d_model), lambda j, i: (i, 0)),
            w_spec(0), w_spec(nj), w_spec(2 * nj), w_spec(3 * nj),
            slab_spec,
        ],
        out_specs=[out_spec] * 4 + [slab_spec],
        out_shape=[out_shape] * 4 + [jax.ShapeDtypeStruct(w_out.shape, BF16)],
        scratch_shapes=[pltpu.VMEM((4, d_model, tn), BF16)],
        compiler_params=_params(("parallel", "arbitrary")),
        name="sb_in_proj",
    )(hn, w_in, w_in, w_in, w_in, w_out)


def _sb_attn_kernel(q_ref, k_ref, v_ref, sg_ref, o_ref, *, tq, nh):
    qi = pl.program_id(2)
    row = lax.broadcasted_iota(jnp.int32, (tq, tq), 0)
    col = lax.broadcasted_iota(jnp.int32, (tq, tq), 1)
    suffix = (row > col).astype(BF16)
    causal = col < row
    lanes = [slice(h * SB_HEAD_DIM, (h + 1) * SB_HEAD_DIM) for h in range(nh)]
    qs = [q_ref[:, lanes[h]] for h in range(nh)]

    def key_rows(kb):
        return pl.ds(pl.multiple_of(kb * tq, tq), tq)

    def logits(kb, h):
        return lax.dot_general(qs[h], k_ref[key_rows(kb), lanes[h]], (((1,), (1,)), ((), ())),
                               preferred_element_type=F32)

    def log_weights(z, masked):
        if masked:
            z = jnp.where(causal, z, NO_WEIGHT)
        log1p_term = jnp.log2(1.0 + jnp.exp2(-jnp.abs(z)))
        log_beta = jnp.minimum(z, 0.0) - log1p_term
        log_fail = log_beta - z
        tail = jnp.dot(log_fail.astype(BF16), suffix, preferred_element_type=F32)
        return log_beta + tail, jnp.sum(log_fail, axis=-1, keepdims=True)

    def weighted_values(kb, h, log_a, carry):
        a = jnp.exp2(log_a if carry is None else log_a + carry)
        return jnp.dot(a.astype(BF16), v_ref[key_rows(kb), lanes[h]], preferred_element_type=F32)

    def block(kb, h, carry):
        log_a, block_sum = log_weights(logits(kb, h), masked=False)
        return block_sum, weighted_values(kb, h, log_a, carry)

    prev = jnp.maximum(qi - 1, 0)
    no_prev = jnp.where(qi > 0, 0.0, NO_WEIGHT)
    units = [(h, kb, masked) for h in range(nh) for kb, masked in ((qi, True), (prev, False))]
    zs, lws, outs = {}, {}, {}
    for t in range(len(units) + 2 * STAGE_SKEW):
        if t < len(units):
            h, kb, _ = units[t]
            zs[t] = logits(kb, h)
        u = t - STAGE_SKEW
        if 0 <= u < len(units):
            lws[u] = log_weights(zs.pop(u), units[u][2])
        u = t - 2 * STAGE_SKEW
        if 0 <= u < len(units):
            h, kb, masked = units[u]
            carry = None if masked else lws[u - 1][1] + no_prev
            outs[u] = weighted_values(kb, h, lws[u][0], carry)
    state = [(lws[2 * h][1] + lws[2 * h + 1][1], outs[2 * h] + outs[2 * h + 1]) for h in range(nh)]

    def live(state):
        top = state[0][0]
        for h in range(1, nh):
            top = jnp.maximum(top, state[h][0])
        return jnp.max(top) >= ZERO_WEIGHT_LOG2

    def cond(loop_state):
        kb, alive, _ = loop_state
        return jnp.logical_and(kb >= 0, alive)

    def body(loop_state):
        kb, _, state = loop_state
        new_state = []
        for h in range(nh):
            carry, acc = state[h]
            block_sum, out = block(kb, h, carry)
            new_state.append((carry + block_sum, acc + out))
        new_state = tuple(new_state)
        return kb - 1, live(new_state), new_state

    state = tuple(state)
    _, _, state = lax.while_loop(cond, body, (qi - 2, live(state), state))
    for h in range(nh):
        o_ref[:, lanes[h]] = (state[h][1] * sg_ref[:, lanes[h]].astype(F32)).astype(BF16)


def _sb_attn(q, k, v, sg, *, batch, seq, tq, nh):
    tokens, width = q.shape
    heads = width // SB_HEAD_DIM
    nq = seq // tq
    grid = (batch, heads // nh, nq)
    q_spec = pl.BlockSpec((tq, nh * SB_HEAD_DIM), lambda b, h, i: (b * nq + i, h))
    kv_spec = pl.BlockSpec((seq, nh * SB_HEAD_DIM), lambda b, h, i: (b, h))
    return pl.pallas_call(
        functools.partial(_sb_attn_kernel, tq=tq, nh=nh),
        grid=grid,
        in_specs=[q_spec, kv_spec, kv_spec, q_spec],
        out_specs=q_spec,
        out_shape=jax.ShapeDtypeStruct((tokens, width), BF16),
        compiler_params=_params(("parallel", "parallel", "arbitrary")),
        name="sb_attention",
    )(q, k, v, sg)


def _sb_out_kernel(h_ref, o_ref, wo_ref, g_ref, out_ref, *, splits):
    for rows in _row_splits(h_ref.shape[0], splits):
        h2 = h_ref[rows, :] + jnp.dot(o_ref[rows, :], wo_ref[...], preferred_element_type=F32)
        out_ref[rows, :] = _rms_normalize(h2, g_ref[...])


def _sb_out(h, og, w_out_bf16, final_g, *, tm, splits):
    tokens, d_model = h.shape
    width = og.shape[1]
    return pl.pallas_call(
        functools.partial(_sb_out_kernel, splits=splits),
        grid=(tokens // tm,),
        in_specs=[
            pl.BlockSpec((tm, d_model), lambda i: (i, 0)),
            pl.BlockSpec((tm, width), lambda i: (i, 0)),
            pl.BlockSpec((width, d_model), lambda i: (0, 0), pipeline_mode=pl.Buffered(1)),
            pl.BlockSpec((1, d_model), lambda i: (0, 0)),
        ],
        out_specs=pl.BlockSpec((tm, d_model), lambda i: (i, 0)),
        out_shape=jax.ShapeDtypeStruct((tokens, d_model), F32),
        compiler_params=_params(("parallel",)),
        name="sb_out_proj_norm",
    )(h, og, w_out_bf16, final_g)


def kernel(x, norm_g, a_w_in, a_v_norm_g, a_w_s, a_b_s, a_w_out, b_w_in, b_w_out, final_g):
    batch, seq, d_model = x.shape
    assert norm_g.shape[0] == 2 and a_w_in.shape[0] == 1 and b_w_in.shape[0] == 1
    x2 = x.reshape(batch * seq, d_model)
    ug, v, rinv, a_w_out_bf16 = _gmlp_in(x2, norm_g[0][None, :], a_w_in[0], a_w_out[0], tm=1024, tn=512)
    h1, hn1 = _gmlp_out(x2, v, ug, rinv, a_v_norm_g[0][None, :], a_w_s[0], a_b_s[0].T,
                        a_w_out_bf16, norm_g[1][None, :], tm=512, splits=2)
    q, k, v, sg, b_w_out_bf16 = _sb_in(hn1, b_w_in[0], b_w_out[0], tm=2048, tn=256, splits=2)
    og = _sb_attn(q, k, v, sg, batch=batch, seq=seq, tq=256, nh=8)
    out = _sb_out(h1, og, b_w_out_bf16, final_g[None, :], tm=1024, splits=4)
    return out.reshape(batch, seq, d_model)
```

```python
import functools

import jax
import jax.numpy as jnp
from jax import lax
from jax.experimental import pallas as pl
from jax.experimental.pallas import tpu as pltpu

EPS = 1e-6
CHUNK = 128
GMLP_GROUPS = 16
SB_HEAD_DIM = 128
GELU_C = 0.7978845608028654
LOG2_E = 1.4426950408889634
ZERO_WEIGHT_LOG2 = -160.0
NO_WEIGHT = -1e30
MATMUL_ROW_SPLITS = 8
STAGE_SKEW = 1

V7X_VMEM_LIMIT_BYTES = 60 * 1024 * 1024

BF16 = jnp.bfloat16
F32 = jnp.float32


def _gelu_tanh(x):
    return 0.5 * x * (1.0 + jnp.tanh(GELU_C * (x + 0.044715 * (x * x * x))))


def _silu(x):
    return 0.5 * x * (1.0 + jnp.tanh(0.5 * x))


def _rms_normalize(x_f32, gain_f32):
    ms = jnp.mean(x_f32 * x_f32, axis=-1, keepdims=True)
    return x_f32 * lax.rsqrt(ms + EPS) * gain_f32


def _row_splits(rows, parts):
    step = rows // parts
    return [slice(r * step, (r + 1) * step) for r in range(parts)]


def _params(semantics):
    return pltpu.CompilerParams(dimension_semantics=semantics,
                                vmem_limit_bytes=V7X_VMEM_LIMIT_BYTES)


def _gmlp_in_kernel(x_ref, g_ref, wu_ref, wv_ref, wg_ref, wo_ref, ug_ref, v_ref, rinv_ref, wo_bf16_ref,
                    xn_ref, ssq_ref, *, width):
    j = pl.program_id(1)
    wo_bf16_ref[...] = wo_ref[...].astype(BF16)

    @pl.when(j == 0)
    def _():
        xn_ref[...] = _rms_normalize(x_ref[...], g_ref[...]).astype(BF16)
        ssq_ref[...] = jnp.zeros_like(ssq_ref)

    wu, wv, wg = (w_ref[...].astype(BF16) for w_ref in (wu_ref, wv_ref, wg_ref))
    for rows in _row_splits(xn_ref.shape[0], MATMUL_ROW_SPLITS):
        xn = xn_ref[rows, :]
        v = _gelu_tanh(jnp.dot(xn, wv, preferred_element_type=F32))
        u = jnp.dot(xn, wu, preferred_element_type=F32)
        zg = jnp.dot(xn, wg, preferred_element_type=F32)
        v_ref[rows, :] = v.astype(BF16)
        ssq_ref[rows, :] += jnp.sum(v * v, axis=-1, keepdims=True)
        ug_ref[rows, :] = (_gelu_tanh(u) * _silu(zg)).astype(BF16)

    @pl.when(j == pl.num_programs(1) - 1)
    def _():
        rinv_ref[...] = lax.rsqrt(ssq_ref[...] * (1.0 / width) + EPS)


def _gmlp_in(x2, gain, w_in, w_out, *, tm, tn):
    tokens, d_model = x2.shape
    width = w_in.shape[1] // 3
    nj = width // tn
    grid = (tokens // tm, nj)
    w_spec = lambda off: pl.BlockSpec((d_model, tn), lambda i, j, off=off: (0, j + off))
    slab = w_out.shape[0] // (grid[0] * grid[1])
    slab_spec = pl.BlockSpec((slab, d_model), lambda i, j: (i * nj + j, 0))
    return pl.pallas_call(
        functools.partial(_gmlp_in_kernel, width=width),
        grid=grid,
        in_specs=[
            pl.BlockSpec((tm, d_model), lambda i, j: (i, 0)),
            pl.BlockSpec((1, d_model), lambda i, j: (0, 0)),
            w_spec(0), w_spec(nj), w_spec(2 * nj),
            slab_spec,
        ],
        out_specs=[
            pl.BlockSpec((tm, tn), lambda i, j: (i, j)),
            pl.BlockSpec((tm, tn), lambda i, j: (i, j)),
            pl.BlockSpec((tm, 1), lambda i, j: (i, 0)),
            slab_spec,
        ],
        out_shape=[
            jax.ShapeDtypeStruct((tokens, width), BF16),
            jax.ShapeDtypeStruct((tokens, width), BF16),
            jax.ShapeDtypeStruct((tokens, 1), F32),
            jax.ShapeDtypeStruct(w_out.shape, BF16),
        ],
        scratch_shapes=[pltpu.VMEM((tm, d_model), BF16), pltpu.VMEM((tm, 1), F32)],
        compiler_params=_params(("parallel", "arbitrary")),
        name="gmlp_in_proj",
    )(x2, gain, w_in, w_in, w_in, w_out)


def _gmlp_out_kernel(x_ref, v_ref, ug_ref, rinv_ref, vg_ref, ws_ref, bs_ref, wo_ref, ng_ref,
                     h_ref, hn_ref, y_ref, *, tm, gd, splits):
    row = lax.broadcasted_iota(jnp.int32, (CHUNK, CHUNK), 0)
    col = lax.broadcasted_iota(jnp.int32, (CHUNK, CHUNK), 1)
    ws = [jnp.where(row >= col, ws_ref[g], 0.0).astype(BF16) for g in range(GMLP_GROUPS)]

    def gate(part):
        for c in range(part.start // CHUNK, part.stop // CHUNK):
            rows = slice(c * CHUNK, (c + 1) * CHUNK)
            rinv = rinv_ref[rows, :]
            for g in range(GMLP_GROUPS):
                cols = slice(g * gd, (g + 1) * gd)
                vn = (v_ref[rows, cols].astype(F32) * rinv * vg_ref[:, cols]).astype(BF16)
                mixed = jnp.dot(ws[g], vn, preferred_element_type=F32) + bs_ref[:, g:g + 1]
                y_ref[rows, cols] = (ug_ref[rows, cols].astype(F32) * mixed).astype(BF16)

    def project(part):
        h = x_ref[part, :] + jnp.dot(y_ref[part, :], wo_ref[...], preferred_element_type=F32)
        h_ref[part, :] = h
        hn_ref[part, :] = _rms_normalize(h, ng_ref[...]).astype(BF16)

    for part in _row_splits(tm, splits):
        gate(part)
        project(part)


def _gmlp_out(x2, v, ug, rinv, v_gain, w_s, b_s_t, w_out_bf16, next_gain, *, tm, splits):
    tokens, d_model = x2.shape
    width = v.shape[1]
    gd = width // GMLP_GROUPS
    row_tile = lambda cols: pl.BlockSpec((tm, cols), lambda i: (i, 0))
    whole = lambda shape: pl.BlockSpec(shape, lambda i: (0,) * len(shape), pipeline_mode=pl.Buffered(1))
    return pl.pallas_call(
        functools.partial(_gmlp_out_kernel, tm=tm, gd=gd, splits=splits),
        grid=(tokens // tm,),
        in_specs=[
            row_tile(d_model), row_tile(width), row_tile(width), row_tile(1),
            whole((1, width)), whole((GMLP_GROUPS, CHUNK, CHUNK)), whole((CHUNK, GMLP_GROUPS)),
            whole((width, d_model)), whole((1, d_model)),
        ],
        out_specs=[row_tile(d_model)] * 2,
        out_shape=[jax.ShapeDtypeStruct((tokens, d_model), F32),
                   jax.ShapeDtypeStruct((tokens, d_model), BF16)],
        scratch_shapes=[pltpu.VMEM((tm, width), BF16)],
        compiler_params=_params(("parallel",)),
        name="gmlp_mix_out_proj",
    )(x2, v, ug, rinv, v_gain, w_s, b_s_t, w_out_bf16, next_gain)


def _sb_in_kernel(hn_ref, wq_ref, wk_ref, wv_ref, wg_ref, wo_ref, q_ref, k_ref, v_ref, sg_ref, wo_bf16_ref,
                  w_bf16_ref, *, splits):
    @pl.when(pl.program_id(1) == 0)
    def _():
        for n, w_ref in enumerate((wq_ref, wk_ref, wv_ref, wg_ref)):
            w_bf16_ref[n] = w_ref[...].astype(BF16)

    wo_bf16_ref[...] = wo_ref[...].astype(BF16)

    for rows in _row_splits(hn_ref.shape[0], splits):
        hn = hn_ref[rows, :]
        dot = lambda n: jnp.dot(hn, w_bf16_ref[n], preferred_element_type=F32)
        q_ref[rows, :] = (dot(0) * (SB_HEAD_DIM ** -0.5 * LOG2_E)).astype(BF16)
        k_ref[rows, :] = dot(1).astype(BF16)
        v_ref[rows, :] = dot(2).astype(BF16)
        sg_ref[rows, :] = _silu(dot(3)).astype(BF16)


def _sb_in(hn, w_in, w_out, *, tm, tn, splits):
    tokens, d_model = hn.shape
    width = w_in.shape[1] // 4
    nj = width // tn
    ni = tokens // tm
    w_spec = lambda off: pl.BlockSpec((d_model, tn), lambda j, i, off=off: (0, j + off))
    out_spec = pl.BlockSpec((tm, tn), lambda j, i: (i, j))
    out_shape = jax.ShapeDtypeStruct((tokens, width), BF16)
    slab = w_out.shape[0] // (nj * ni)
    slab_spec = pl.BlockSpec((slab, w_out.shape[1]), lambda j, i: (j * ni + i, 0))
    return pl.pallas_call(
        functools.partial(_sb_in_kernel, splits=splits),
        grid=(nj, ni),
        in_specs=[
            pl.BlockSpec((tm, d_model), lambda j, i: (i, 0)),
            w_spec(0), w_spec(nj), w_spec(2 * nj), w_spec(3 * nj),
            slab_spec,
        ],
        out_specs=[out_spec] * 4 + [slab_spec],
        out_shape=[out_shape] * 4 + [jax.ShapeDtypeStruct(w_out.shape, BF16)],
        scratch_shapes=[pltpu.VMEM((4, d_model, tn), BF16)],
        compiler_params=_params(("parallel", "arbitrary")),
        name="sb_in_proj",
    )(hn, w_in, w_in, w_in, w_in, w_out)


def _sb_attn_kernel(q_ref, k_ref, v_ref, sg_ref, o_ref, carry_ref, acc_ref, *, tq, nh, nqb):
    row = lax.broadcasted_iota(jnp.int32, (tq, tq), 0)
    col = lax.broadcasted_iota(jnp.int32, (tq, tq), 1)
    suffix = (row > col).astype(BF16)
    causal = col < row
    lanes = [slice(h * SB_HEAD_DIM, (h + 1) * SB_HEAD_DIM) for h in range(nh)]
    q_rows = _row_splits(nqb * tq, nqb)
    q_blocks = [pl.program_id(2) * nqb + s for s in range(nqb)]

    def key_rows(kb):
        return pl.ds(pl.multiple_of(kb * tq, tq), tq)

    def logits(s, kb, h):
        return lax.dot_general(q_ref[q_rows[s], lanes[h]], k_ref[key_rows(kb), lanes[h]],
                               (((1,), (1,)), ((), ())), preferred_element_type=F32)

    def log_weights(z, masked):
        if masked:
            z = jnp.where(causal, z, NO_WEIGHT)
        log1p_term = jnp.log2(1.0 + jnp.exp2(-jnp.abs(z)))
        log_beta = jnp.minimum(z, 0.0) - log1p_term
        log_fail = log_beta - z
        tail = jnp.dot(log_fail.astype(BF16), suffix, preferred_element_type=F32)
        return log_beta + tail, jnp.sum(log_fail, axis=-1, keepdims=True)

    def weighted_values(kb, h, log_a, carry):
        a = jnp.exp2(log_a if carry is None else log_a + carry)
        return jnp.dot(a.astype(BF16), v_ref[key_rows(kb), lanes[h]], preferred_element_type=F32)

    units = []
    for s in range(nqb):
        prev = jnp.maximum(q_blocks[s] - 1, 0)
        no_prev = jnp.where(q_blocks[s] > 0, 0.0, NO_WEIGHT)
        for h in range(nh):
            units += [(s, h, q_blocks[s], True, None), (s, h, prev, False, no_prev)]
    zs, lws, outs = {}, {}, {}
    for t in range(len(units) + 2 * STAGE_SKEW):
        if t < len(units):
            s, h, kb, _, _ = units[t]
            zs[t] = logits(s, kb, h)
        u = t - STAGE_SKEW
        if 0 <= u < len(units):
            lws[u] = log_weights(zs.pop(u), units[u][3])
        u = t - 2 * STAGE_SKEW
        if 0 <= u < len(units):
            _, h, kb, masked, no_prev = units[u]
            carry = None if masked else lws[u - 1][1] + no_prev
            outs[u] = weighted_values(kb, h, lws[u][0], carry)

    def any_live(carries):
        top = carries[0]
        for carry in carries[1:]:
            top = jnp.maximum(top, carry)
        return jnp.max(top) >= ZERO_WEIGHT_LOG2

    for s in range(nqb):
        first = 2 * nh * s
        carries = [lws[first + 2 * h][1] + lws[first + 2 * h + 1][1] for h in range(nh)]
        for h in range(nh):
            carry_ref[s, h] = carries[h]
            acc_ref[s, h] = outs[first + 2 * h] + outs[first + 2 * h + 1]
        alive = any_live(carries)

        @pl.when(jnp.logical_and(q_blocks[s] >= 2, alive))
        def _(s=s, alive=alive):
            def cond(loop_state):
                kb, alive = loop_state
                return jnp.logical_and(kb >= 0, alive)

            def body(loop_state):
                kb, _ = loop_state
                carries = []
                for h in range(nh):
                    carry = carry_ref[s, h]
                    log_a, block_sum = log_weights(logits(s, kb, h), masked=False)
                    acc_ref[s, h] += weighted_values(kb, h, log_a, carry)
                    carries.append(carry + block_sum)
                    carry_ref[s, h] = carries[h]
                return kb - 1, any_live(carries)

            lax.while_loop(cond, body, (q_blocks[s] - 2, alive))

        for h in range(nh):
            o_ref[q_rows[s], lanes[h]] = (acc_ref[s, h] * sg_ref[q_rows[s], lanes[h]].astype(F32)).astype(BF16)


def _sb_attn(q, k, v, sg, *, batch, seq, tq, nh, nqb):
    tokens, width = q.shape
    heads = width // SB_HEAD_DIM
    steps = seq // (tq * nqb)
    grid = (batch, heads // nh, steps)
    q_spec = pl.BlockSpec((nqb * tq, nh * SB_HEAD_DIM), lambda b, h, i: (b * steps + i, h))
    kv_spec = pl.BlockSpec((seq, nh * SB_HEAD_DIM), lambda b, h, i: (b, h))
    return pl.pallas_call(
        functools.partial(_sb_attn_kernel, tq=tq, nh=nh, nqb=nqb),
        grid=grid,
        in_specs=[q_spec, kv_spec, kv_spec, q_spec],
        out_specs=q_spec,
        out_shape=jax.ShapeDtypeStruct((tokens, width), BF16),
        scratch_shapes=[pltpu.VMEM((nqb, nh, tq, 1), F32), pltpu.VMEM((nqb, nh, tq, SB_HEAD_DIM), F32)],
        compiler_params=_params(("parallel", "parallel", "arbitrary")),
        name="sb_attention",
    )(q, k, v, sg)


def _sb_out_kernel(h_ref, o_ref, wo_ref, g_ref, out_ref, *, splits):
    for rows in _row_splits(h_ref.shape[0], splits):
        h2 = h_ref[rows, :] + jnp.dot(o_ref[rows, :], wo_ref[...], preferred_element_type=F32)
        out_ref[rows, :] = _rms_normalize(h2, g_ref[...])


def _sb_out(h, og, w_out_bf16, final_g, *, tm, splits):
    tokens, d_model = h.shape
    width = og.shape[1]
    return pl.pallas_call(
        functools.partial(_sb_out_kernel, splits=splits),
        grid=(tokens // tm,),
        in_specs=[
            pl.BlockSpec((tm, d_model), lambda i: (i, 0)),
            pl.BlockSpec((tm, width), lambda i: (i, 0)),
            pl.BlockSpec((width, d_model), lambda i: (0, 0), pipeline_mode=pl.Buffered(1)),
            pl.BlockSpec((1, d_model), lambda i: (0, 0)),
        ],
        out_specs=pl.BlockSpec((tm, d_model), lambda i: (i, 0)),
        out_shape=jax.ShapeDtypeStruct((tokens, d_model), F32),
        compiler_params=_params(("parallel",)),
        name="sb_out_proj_norm",
    )(h, og, w_out_bf16, final_g)


def kernel(x, norm_g, a_w_in, a_v_norm_g, a_w_s, a_b_s, a_w_out, b_w_in, b_w_out, final_g):
    batch, seq, d_model = x.shape
    assert norm_g.shape[0] == 2 and a_w_in.shape[0] == 1 and b_w_in.shape[0] == 1
    x2 = x.reshape(batch * seq, d_model)
    ug, v, rinv, a_w_out_bf16 = _gmlp_in(x2, norm_g[0][None, :], a_w_in[0], a_w_out[0], tm=1024, tn=512)
    h1, hn1 = _gmlp_out(x2, v, ug, rinv, a_v_norm_g[0][None, :], a_w_s[0], a_b_s[0].T,
                        a_w_out_bf16, norm_g[1][None, :], tm=512, splits=2)
    q, k, v, sg, b_w_out_bf16 = _sb_in(hn1, b_w_in[0], b_w_out[0], tm=2048, tn=256, splits=2)
    og = _sb_attn(q, k, v, sg, batch=batch, seq=seq, tq=256, nh=8, nqb=2)
    out = _sb_out(h1, og, b_w_out_bf16, final_g[None, :], tm=1024, splits=4)
    return out.reshape(batch, seq, d_model)
```

```python
import functools

import jax
import jax.numpy as jnp
from jax import lax
from jax.experimental import pallas as pl
from jax.experimental.pallas import tpu as pltpu

EPS = 1e-6
CHUNK = 128
GMLP_GROUPS = 16
SB_HEAD_DIM = 128
GELU_C = 0.7978845608028654
LOG2_E = 1.4426950408889634
ZERO_WEIGHT_LOG2 = -160.0
NO_WEIGHT = -1e30
MATMUL_ROW_SPLITS = 8
STAGE_SKEW = 1

V7X_VMEM_LIMIT_BYTES = 60 * 1024 * 1024

BF16 = jnp.bfloat16
F32 = jnp.float32


def _gelu_tanh(x):
    return 0.5 * x * (1.0 + jnp.tanh(GELU_C * (x + 0.044715 * (x * x * x))))


def _silu(x):
    return 0.5 * x * (1.0 + jnp.tanh(0.5 * x))


def _rms_normalize(x_f32, gain_f32):
    ms = jnp.mean(x_f32 * x_f32, axis=-1, keepdims=True)
    return x_f32 * lax.rsqrt(ms + EPS) * gain_f32


def _row_splits(rows, parts):
    step = rows // parts
    return [slice(r * step, (r + 1) * step) for r in range(parts)]


def _params(semantics):
    return pltpu.CompilerParams(dimension_semantics=semantics,
                                vmem_limit_bytes=V7X_VMEM_LIMIT_BYTES)


def _gmlp_in_kernel(*refs, width, first_tile):
    if first_tile:
        (x_ref, g_ref, wu_ref, wv_ref, wg_ref,
         ug_ref, v_ref, rinv_ref, wu_bf16_ref, wv_bf16_ref, wg_bf16_ref, xn_ref, ssq_ref) = refs
    else:
        (x_ref, g_ref, wu_ref, wv_ref, wg_ref, wo_ref, _, _, _,
         ug_ref, v_ref, rinv_ref, wo_bf16_ref, xn_ref, ssq_ref) = refs
        wo_bf16_ref[...] = wo_ref[...].astype(BF16)
    j = pl.program_id(1)

    def column_step(first):
        wu, wv, wg = (w_ref[...].astype(BF16) for w_ref in (wu_ref, wv_ref, wg_ref))
        if first_tile:
            wu_bf16_ref[...], wv_bf16_ref[...], wg_bf16_ref[...] = wu, wv, wg
        for rows in _row_splits(xn_ref.shape[0], MATMUL_ROW_SPLITS):
            if first:
                xn = _rms_normalize(x_ref[rows, :], g_ref[...]).astype(BF16)
                xn_ref[rows, :] = xn
            else:
                xn = xn_ref[rows, :]
            v = _gelu_tanh(jnp.dot(xn, wv, preferred_element_type=F32))
            u = jnp.dot(xn, wu, preferred_element_type=F32)
            zg = jnp.dot(xn, wg, preferred_element_type=F32)
            v_ref[rows, :] = v.astype(BF16)
            ssq = jnp.sum(v * v, axis=-1, keepdims=True)
            ssq_ref[rows, :] = ssq if first else ssq_ref[rows, :] + ssq
            ug_ref[rows, :] = (_gelu_tanh(u) * _silu(zg)).astype(BF16)

    pl.when(j == 0)(functools.partial(column_step, True))
    pl.when(j > 0)(functools.partial(column_step, False))

    @pl.when(j == pl.num_programs(1) - 1)
    def _():
        rinv_ref[...] = lax.rsqrt(ssq_ref[...] * (1.0 / width) + EPS)


def _gmlp_in(x2, gain, w_in, w_out, *, tm, tn):
    tokens, d_model = x2.shape
    width = w_in.shape[1] // 3
    nj = width // tn
    n_tiles = tokens // tm
    act_shapes = [jax.ShapeDtypeStruct((tokens, width), BF16), jax.ShapeDtypeStruct((tokens, width), BF16),
                  jax.ShapeDtypeStruct((tokens, 1), F32)]

    def specs(tile0, tn):
        x_spec = pl.BlockSpec((tm, d_model), lambda i, j: (i + tile0, 0))
        g_spec = pl.BlockSpec((1, d_model), lambda i, j: (0, 0))
        act_specs = [pl.BlockSpec((tm, tn), lambda i, j: (i + tile0, j)),
                     pl.BlockSpec((tm, tn), lambda i, j: (i + tile0, j)),
                     pl.BlockSpec((tm, 1), lambda i, j: (i + tile0, 0))]
        w_spec = lambda off: pl.BlockSpec((d_model, tn), lambda i, j, off=off: (0, j + off))
        return x_spec, g_spec, act_specs, w_spec

    scratch = [pltpu.VMEM((tm, d_model), BF16), pltpu.VMEM((tm, 1), F32)]

    nj0 = 2 * nj
    x_spec, g_spec, act_specs, w_spec = specs(0, tn // 2)
    ug, v, rinv, wu, wv, wg = pl.pallas_call(
        functools.partial(_gmlp_in_kernel, width=width, first_tile=True),
        grid=(1, nj0),
        in_specs=[x_spec, g_spec, w_spec(0), w_spec(nj0), w_spec(2 * nj0)],
        out_specs=act_specs + [w_spec(0)] * 3,
        out_shape=act_shapes + [jax.ShapeDtypeStruct((d_model, width), BF16)] * 3,
        scratch_shapes=scratch,
        compiler_params=_params(("parallel", "arbitrary")),
        name="gmlp_in_proj_tile0",
    )(x2, gain, w_in, w_in, w_in)

    x_spec, g_spec, act_specs, w_spec = specs(1, tn)
    carried = pl.BlockSpec(memory_space=pl.ANY)
    n_slabs = w_out.shape[0] // CHUNK
    assert n_slabs <= (n_tiles - 1) * nj
    slab_spec = pl.BlockSpec((CHUNK, d_model), lambda i, j: (jnp.minimum(i * nj + j, n_slabs - 1), 0))
    ug, v, rinv, w_out_bf16 = pl.pallas_call(
        functools.partial(_gmlp_in_kernel, width=width, first_tile=False),
        grid=(n_tiles - 1, nj),
        in_specs=[x_spec, g_spec, w_spec(0), w_spec(0), w_spec(0), slab_spec, carried, carried, carried],
        out_specs=act_specs + [slab_spec],
        out_shape=act_shapes + [jax.ShapeDtypeStruct(w_out.shape, BF16)],
        input_output_aliases={6: 0, 7: 1, 8: 2},
        scratch_shapes=scratch,
        compiler_params=_params(("arbitrary", "arbitrary")),
        name="gmlp_in_proj",
    )(x2, gain, wu, wv, wg, w_out, ug, v, rinv)
    return ug, v, rinv, w_out_bf16


def _gmlp_out_kernel(x_ref, v_ref, ug_ref, rinv_ref, vg_ref, ws_ref, bs_ref, wo_ref, ng_ref,
                     h_ref, hn_ref, y_ref, *, tm, gd, splits):
    row = lax.broadcasted_iota(jnp.int32, (CHUNK, CHUNK), 0)
    col = lax.broadcasted_iota(jnp.int32, (CHUNK, CHUNK), 1)
    ws = [jnp.where(row >= col, ws_ref[g], 0.0).astype(BF16) for g in range(GMLP_GROUPS)]

    def gate(part):
        for c in range(part.start // CHUNK, part.stop // CHUNK):
            rows = slice(c * CHUNK, (c + 1) * CHUNK)
            rinv = rinv_ref[rows, :]
            for g in range(GMLP_GROUPS):
                cols = slice(g * gd, (g + 1) * gd)
                vn = (v_ref[rows, cols].astype(F32) * rinv * vg_ref[:, cols]).astype(BF16)
                mixed = jnp.dot(ws[g], vn, preferred_element_type=F32) + bs_ref[:, g:g + 1]
                y_ref[rows, cols] = (ug_ref[rows, cols].astype(F32) * mixed).astype(BF16)

    def project(part):
        h = x_ref[part, :] + jnp.dot(y_ref[part, :], wo_ref[...], preferred_element_type=F32)
        h_ref[part, :] = h
        hn_ref[part, :] = _rms_normalize(h, ng_ref[...]).astype(BF16)

    for part in _row_splits(tm, splits):
        gate(part)
        project(part)


def _gmlp_out(x2, v, ug, rinv, v_gain, w_s, b_s_t, w_out_bf16, next_gain, *, tm, splits):
    tokens, d_model = x2.shape
    width = v.shape[1]
    gd = width // GMLP_GROUPS
    row_tile = lambda cols: pl.BlockSpec((tm, cols), lambda i: (i, 0))
    whole = lambda shape: pl.BlockSpec(shape, lambda i: (0,) * len(shape), pipeline_mode=pl.Buffered(1))
    return pl.pallas_call(
        functools.partial(_gmlp_out_kernel, tm=tm, gd=gd, splits=splits),
        grid=(tokens // tm,),
        in_specs=[
            row_tile(d_model), row_tile(width), row_tile(width), row_tile(1),
            whole((1, width)), whole((GMLP_GROUPS, CHUNK, CHUNK)), whole((CHUNK, GMLP_GROUPS)),
            whole((width, d_model)), whole((1, d_model)),
        ],
        out_specs=[row_tile(d_model)] * 2,
        out_shape=[jax.ShapeDtypeStruct((tokens, d_model), F32),
                   jax.ShapeDtypeStruct((tokens, d_model), BF16)],
        scratch_shapes=[pltpu.VMEM((tm, width), BF16)],
        compiler_params=_params(("parallel",)),
        name="gmlp_mix_out_proj",
    )(x2, v, ug, rinv, v_gain, w_s, b_s_t, w_out_bf16, next_gain)


def _sb_in_kernel(hn_ref, wq_ref, wk_ref, wv_ref, wg_ref, wo_ref, q_ref, k_ref, v_ref, sg_ref, wo_bf16_ref,
                  w_bf16_ref, *, splits):
    @pl.when(pl.program_id(1) == 0)
    def _():
        for n, w_ref in enumerate((wq_ref, wk_ref, wv_ref, wg_ref)):
            w_bf16_ref[n] = w_ref[...].astype(BF16)

    wo_bf16_ref[...] = wo_ref[...].astype(BF16)

    for rows in _row_splits(hn_ref.shape[0], splits):
        hn = hn_ref[rows, :]
        dot = lambda n: jnp.dot(hn, w_bf16_ref[n], preferred_element_type=F32)
        q_ref[rows, :] = (dot(0) * (SB_HEAD_DIM ** -0.5 * LOG2_E)).astype(BF16)
        k_ref[rows, :] = dot(1).astype(BF16)
        v_ref[rows, :] = dot(2).astype(BF16)
        sg_ref[rows, :] = _silu(dot(3)).astype(BF16)


def _sb_in(hn, w_in, w_out, *, tm, tn, splits):
    tokens, d_model = hn.shape
    width = w_in.shape[1] // 4
    nj = width // tn
    ni = tokens // tm
    w_spec = lambda off: pl.BlockSpec((d_model, tn), lambda j, i, off=off: (0, j + off))
    out_spec = pl.BlockSpec((tm, tn), lambda j, i: (i, j))
    out_shape = jax.ShapeDtypeStruct((tokens, width), BF16)
    slab = w_out.shape[0] // (nj * ni)
    slab_spec = pl.BlockSpec((slab, w_out.shape[1]), lambda j, i: (j * ni + i, 0))
    return pl.pallas_call(
        functools.partial(_sb_in_kernel, splits=splits),
        grid=(nj, ni),
        in_specs=[
            pl.BlockSpec((tm, d_model), lambda j, i: (i, 0)),
            w_spec(0), w_spec(nj), w_spec(2 * nj), w_spec(3 * nj),
            slab_spec,
        ],
        out_specs=[out_spec] * 4 + [slab_spec],
        out_shape=[out_shape] * 4 + [jax.ShapeDtypeStruct(w_out.shape, BF16)],
        scratch_shapes=[pltpu.VMEM((4, d_model, tn), BF16)],
        compiler_params=_params(("parallel", "arbitrary")),
        name="sb_in_proj",
    )(hn, w_in, w_in, w_in, w_in, w_out)


def _sb_attn_kernel(q_ref, k_ref, v_ref, sg_ref, o_ref, carry_ref, acc_ref, *, tq, nh, nqb):
    row = lax.broadcasted_iota(jnp.int32, (tq, tq), 0)
    col = lax.broadcasted_iota(jnp.int32, (tq, tq), 1)
    suffix = (row > col).astype(BF16)
    causal = col < row
    lanes = [slice(h * SB_HEAD_DIM, (h + 1) * SB_HEAD_DIM) for h in range(nh)]
    q_rows = _row_splits(nqb * tq, nqb)
    q_blocks = [pl.program_id(2) * nqb + s for s in range(nqb)]

    def key_rows(kb):
        return pl.ds(pl.multiple_of(kb * tq, tq), tq)

    def logits(s, kb, h):
        return lax.dot_general(q_ref[q_rows[s], lanes[h]], k_ref[key_rows(kb), lanes[h]],
                               (((1,), (1,)), ((), ())), preferred_element_type=F32)

    def log_weights(z, masked):
        if masked:
            z = jnp.where(causal, z, NO_WEIGHT)
        log1p_term = jnp.log2(1.0 + jnp.exp2(-jnp.abs(z)))
        log_beta = jnp.minimum(z, 0.0) - log1p_term
        log_fail = log_beta - z
        tail = jnp.dot(log_fail.astype(BF16), suffix, preferred_element_type=F32)
        return log_beta + tail, jnp.sum(log_fail, axis=-1, keepdims=True)

    def weighted_values(kb, h, log_a, carry):
        a = jnp.exp2(log_a if carry is None else log_a + carry)
        return jnp.dot(a.astype(BF16), v_ref[key_rows(kb), lanes[h]], preferred_element_type=F32)

    units = []
    for s in range(nqb):
        prev = jnp.maximum(q_blocks[s] - 1, 0)
        no_prev = jnp.where(q_blocks[s] > 0, 0.0, NO_WEIGHT)
        for h in range(nh):
            units += [(s, h, q_blocks[s], True, None), (s, h, prev, False, no_prev)]
    zs, lws, outs = {}, {}, {}
    for t in range(len(units) + 2 * STAGE_SKEW):
        if t < len(units):
            s, h, kb, _, _ = units[t]
            zs[t] = logits(s, kb, h)
        u = t - STAGE_SKEW
        if 0 <= u < len(units):
            lws[u] = log_weights(zs.pop(u), units[u][3])
        u = t - 2 * STAGE_SKEW
        if 0 <= u < len(units):
            _, h, kb, masked, no_prev = units[u]
            carry = None if masked else lws[u - 1][1] + no_prev
            outs[u] = weighted_values(kb, h, lws[u][0], carry)

    def any_live(carries):
        top = carries[0]
        for carry in carries[1:]:
            top = jnp.maximum(top, carry)
        return jnp.max(top) >= ZERO_WEIGHT_LOG2

    for s in range(nqb):
        first = 2 * nh * s
        carries = [lws[first + 2 * h][1] + lws[first + 2 * h + 1][1] for h in range(nh)]
        for h in range(nh):
            carry_ref[s, h] = carries[h]
            acc_ref[s, h] = outs[first + 2 * h] + outs[first + 2 * h + 1]
        alive = any_live(carries)

        @pl.when(jnp.logical_and(q_blocks[s] >= 2, alive))
        def _(s=s, alive=alive):
            def cond(loop_state):
                kb, alive = loop_state
                return jnp.logical_and(kb >= 0, alive)

            def body(loop_state):
                kb, _ = loop_state
                carries = []
                for h in range(nh):
                    carry = carry_ref[s, h]
                    log_a, block_sum = log_weights(logits(s, kb, h), masked=False)
                    acc_ref[s, h] += weighted_values(kb, h, log_a, carry)
                    carries.append(carry + block_sum)
                    carry_ref[s, h] = carries[h]
                return kb - 1, any_live(carries)

            lax.while_loop(cond, body, (q_blocks[s] - 2, alive))

        for h in range(nh):
            o_ref[q_rows[s], lanes[h]] = (acc_ref[s, h] * sg_ref[q_rows[s], lanes[h]].astype(F32)).astype(BF16)


def _sb_attn(q, k, v, sg, *, batch, seq, tq, nh, nqb):
    tokens, width = q.shape
    heads = width // SB_HEAD_DIM
    steps = seq // (tq * nqb)
    grid = (batch, heads // nh, steps)
    q_spec = pl.BlockSpec((nqb * tq, nh * SB_HEAD_DIM), lambda b, h, i: (b * steps + i, h))
    kv_spec = pl.BlockSpec((seq, nh * SB_HEAD_DIM), lambda b, h, i: (b, h))
    return pl.pallas_call(
        functools.partial(_sb_attn_kernel, tq=tq, nh=nh, nqb=nqb),
        grid=grid,
        in_specs=[q_spec, kv_spec, kv_spec, q_spec],
        out_specs=q_spec,
        out_shape=jax.ShapeDtypeStruct((tokens, width), BF16),
        scratch_shapes=[pltpu.VMEM((nqb, nh, tq, 1), F32), pltpu.VMEM((nqb, nh, tq, SB_HEAD_DIM), F32)],
        compiler_params=_params(("parallel", "parallel", "arbitrary")),
        name="sb_attention",
    )(q, k, v, sg)


def _sb_out_kernel(h_ref, o_ref, wo_ref, g_ref, out_ref, *, splits):
    for rows in _row_splits(h_ref.shape[0], splits):
        h2 = h_ref[rows, :] + jnp.dot(o_ref[rows, :], wo_ref[...], preferred_element_type=F32)
        out_ref[rows, :] = _rms_normalize(h2, g_ref[...])


def _sb_out(h, og, w_out_bf16, final_g, *, tm, splits):
    tokens, d_model = h.shape
    width = og.shape[1]
    return pl.pallas_call(
        functools.partial(_sb_out_kernel, splits=splits),
        grid=(tokens // tm,),
        in_specs=[
            pl.BlockSpec((tm, d_model), lambda i: (i, 0)),
            pl.BlockSpec((tm, width), lambda i: (i, 0)),
            pl.BlockSpec((width, d_model), lambda i: (0, 0), pipeline_mode=pl.Buffered(1)),
            pl.BlockSpec((1, d_model), lambda i: (0, 0)),
        ],
        out_specs=pl.BlockSpec((tm, d_model), lambda i: (i, 0)),
        out_shape=jax.ShapeDtypeStruct((tokens, d_model), F32),
        compiler_params=_params(("parallel",)),
        name="sb_out_proj_norm",
    )(h, og, w_out_bf16, final_g)


def kernel(x, norm_g, a_w_in, a_v_norm_g, a_w_s, a_b_s, a_w_out, b_w_in, b_w_out, final_g):
    batch, seq, d_model = x.shape
    assert norm_g.shape[0] == 2 and a_w_in.shape[0] == 1 and b_w_in.shape[0] == 1
    x2 = x.reshape(batch * seq, d_model)
    ug, v, rinv, a_w_out_bf16 = _gmlp_in(x2, norm_g[0][None, :], a_w_in[0], a_w_out[0], tm=1024, tn=512)
    h1, hn1 = _gmlp_out(x2, v, ug, rinv, a_v_norm_g[0][None, :], a_w_s[0], a_b_s[0].T,
                        a_w_out_bf16, norm_g[1][None, :], tm=512, splits=2)
    q, k, v, sg, b_w_out_bf16 = _sb_in(hn1, b_w_in[0], b_w_out[0], tm=2048, tn=256, splits=2)
    og = _sb_attn(q, k, v, sg, batch=batch, seq=seq, tq=256, nh=8, nqb=2)
    out = _sb_out(h1, og, b_w_out_bf16, final_g[None, :], tm=1024, splits=4)
    return out.reshape(batch, seq, d_model)
```

```python
import functools

import jax
import jax.numpy as jnp
from jax import lax
from jax.experimental import pallas as pl
from jax.experimental.pallas import tpu as pltpu

EPS = 1e-6
CHUNK = 128
GMLP_GROUPS = 16
SB_HEAD_DIM = 128
GELU_C = 0.7978845608028654
LOG2_E = 1.4426950408889634
ZERO_WEIGHT_LOG2 = -160.0
NO_WEIGHT = -1e30
MATMUL_ROW_SPLITS = 8
WEIGHT_COPY_CHUNKS = 4
STAGE_SKEW = 1

V7X_VMEM_LIMIT_BYTES = 60 * 1024 * 1024

BF16 = jnp.bfloat16
F32 = jnp.float32


def _gelu_tanh(x):
    return 0.5 * x * (1.0 + jnp.tanh(GELU_C * (x + 0.044715 * (x * x * x))))


def _silu(x):
    return 0.5 * x * (1.0 + jnp.tanh(0.5 * x))


def _rms_normalize(x_f32, gain_f32):
    ms = jnp.mean(x_f32 * x_f32, axis=-1, keepdims=True)
    return x_f32 * lax.rsqrt(ms + EPS) * gain_f32


def _row_splits(rows, parts):
    step = rows // parts
    return [slice(r * step, (r + 1) * step) for r in range(parts)]


def _params(semantics):
    return pltpu.CompilerParams(dimension_semantics=semantics,
                                vmem_limit_bytes=V7X_VMEM_LIMIT_BYTES)


def _gmlp_in_kernel(x_ref, g_ref, wu_ref, wv_ref, wg_ref, wo_ref, ug_ref, v_ref, rinv_ref, wo_bf16_ref,
                    xn_ref, ssq_ref, *, width):
    j = pl.program_id(1)
    wo_bf16_ref[...] = wo_ref[...].astype(BF16)

    def column_step(first):
        wu, wv, wg = (w_ref[...].astype(BF16) for w_ref in (wu_ref, wv_ref, wg_ref))
        for rows in _row_splits(xn_ref.shape[0], MATMUL_ROW_SPLITS):
            if first:
                xn = _rms_normalize(x_ref[rows, :], g_ref[...]).astype(BF16)
                xn_ref[rows, :] = xn
            else:
                xn = xn_ref[rows, :]
            v = _gelu_tanh(jnp.dot(xn, wv, preferred_element_type=F32))
            u = jnp.dot(xn, wu, preferred_element_type=F32)
            zg = jnp.dot(xn, wg, preferred_element_type=F32)
            v_ref[rows, :] = v.astype(BF16)
            ssq = jnp.sum(v * v, axis=-1, keepdims=True)
            ssq_ref[rows, :] = ssq if first else ssq_ref[rows, :] + ssq
            ug_ref[rows, :] = (_gelu_tanh(u) * _silu(zg)).astype(BF16)

    pl.when(j == 0)(functools.partial(column_step, True))
    pl.when(j > 0)(functools.partial(column_step, False))

    @pl.when(j == pl.num_programs(1) - 1)
    def _():
        rinv_ref[...] = lax.rsqrt(ssq_ref[...] * (1.0 / width) + EPS)


def _gmlp_in(x2, gain, w_in, w_out, *, tm, tn):
    tokens, d_model = x2.shape
    width = w_in.shape[1] // 3
    nj = width // tn
    grid = (tokens // tm, nj)
    w_spec = lambda off: pl.BlockSpec((d_model, tn), lambda i, j, off=off: (0, j + off))
    slab = w_out.shape[0] // (grid[0] * grid[1])
    slab_spec = pl.BlockSpec((slab, d_model), lambda i, j: (i * nj + j, 0))
    return pl.pallas_call(
        functools.partial(_gmlp_in_kernel, width=width),
        grid=grid,
        in_specs=[
            pl.BlockSpec((tm, d_model), lambda i, j: (i, 0)),
            pl.BlockSpec((1, d_model), lambda i, j: (0, 0)),
            w_spec(0), w_spec(nj), w_spec(2 * nj),
            slab_spec,
        ],
        out_specs=[
            pl.BlockSpec((tm, tn), lambda i, j: (i, j)),
            pl.BlockSpec((tm, tn), lambda i, j: (i, j)),
            pl.BlockSpec((tm, 1), lambda i, j: (i, 0)),
            slab_spec,
        ],
        out_shape=[
            jax.ShapeDtypeStruct((tokens, width), BF16),
            jax.ShapeDtypeStruct((tokens, width), BF16),
            jax.ShapeDtypeStruct((tokens, 1), F32),
            jax.ShapeDtypeStruct(w_out.shape, BF16),
        ],
        scratch_shapes=[pltpu.VMEM((tm, d_model), BF16), pltpu.VMEM((tm, 1), F32)],
        compiler_params=_params(("parallel", "arbitrary")),
        name="gmlp_in_proj",
    )(x2, gain, w_in, w_in, w_in, w_out)


def _gmlp_out_kernel(x_ref, v_ref, ug_ref, rinv_ref, vg_ref, ws_ref, bs_ref, wo_hbm_ref, ng_ref,
                     h_ref, hn_ref, y_ref, wo_ref, wo_sem, *, tm, gd, splits):
    col_chunks = _row_splits(h_ref.shape[1], WEIGHT_COPY_CHUNKS)

    def weight_copy(c):
        return pltpu.make_async_copy(wo_hbm_ref.at[:, col_chunks[c]], wo_ref.at[:, col_chunks[c]], wo_sem.at[c])

    def token_step(first):
        if first:
            for c in range(WEIGHT_COPY_CHUNKS):
                weight_copy(c).start()
        row = lax.broadcasted_iota(jnp.int32, (CHUNK, CHUNK), 0)
        col = lax.broadcasted_iota(jnp.int32, (CHUNK, CHUNK), 1)
        ws = [jnp.where(row >= col, ws_ref[g], 0.0).astype(BF16) for g in range(GMLP_GROUPS)]
        for s, part in enumerate(_row_splits(tm, splits)):
            for c in range(part.start // CHUNK, part.stop // CHUNK):
                rows = slice(c * CHUNK, (c + 1) * CHUNK)
                rinv = rinv_ref[rows, :]
                for g in range(GMLP_GROUPS):
                    cols = slice(g * gd, (g + 1) * gd)
                    vn = (v_ref[rows, cols].astype(F32) * rinv * vg_ref[:, cols]).astype(BF16)
                    mixed = jnp.dot(ws[g], vn, preferred_element_type=F32) + bs_ref[:, g:g + 1]
                    y_ref[rows, cols] = (ug_ref[rows, cols].astype(F32) * mixed).astype(BF16)
            for c, cols in enumerate(col_chunks):
                if first and s == 0:
                    weight_copy(c).wait()
                h_ref[part, cols] = x_ref[part, cols] + jnp.dot(y_ref[part, :], wo_ref[:, cols],
                                                                preferred_element_type=F32)
            hn_ref[part, :] = _rms_normalize(h_ref[part, :], ng_ref[...]).astype(BF16)

    pl.when(pl.program_id(0) == 0)(functools.partial(token_step, True))
    pl.when(pl.program_id(0) > 0)(functools.partial(token_step, False))


def _gmlp_out(x2, v, ug, rinv, v_gain, w_s, b_s_t, w_out_bf16, next_gain, *, tm, splits):
    tokens, d_model = x2.shape
    width = v.shape[1]
    gd = width // GMLP_GROUPS
    row_tile = lambda cols: pl.BlockSpec((tm, cols), lambda i: (i, 0))
    whole = lambda shape: pl.BlockSpec(shape, lambda i: (0,) * len(shape), pipeline_mode=pl.Buffered(1))
    return pl.pallas_call(
        functools.partial(_gmlp_out_kernel, tm=tm, gd=gd, splits=splits),
        grid=(tokens // tm,),
        in_specs=[
            row_tile(d_model), row_tile(width), row_tile(width), row_tile(1),
            whole((1, width)), whole((GMLP_GROUPS, CHUNK, CHUNK)), whole((CHUNK, GMLP_GROUPS)),
            pl.BlockSpec(memory_space=pl.ANY), whole((1, d_model)),
        ],
        out_specs=[row_tile(d_model)] * 2,
        out_shape=[jax.ShapeDtypeStruct((tokens, d_model), F32),
                   jax.ShapeDtypeStruct((tokens, d_model), BF16)],
        scratch_shapes=[pltpu.VMEM((tm, width), BF16), pltpu.VMEM((width, d_model), BF16),
                        pltpu.SemaphoreType.DMA((WEIGHT_COPY_CHUNKS,))],
        compiler_params=_params(("arbitrary",)),
        name="gmlp_mix_out_proj",
    )(x2, v, ug, rinv, v_gain, w_s, b_s_t, w_out_bf16, next_gain)


def _sb_in_kernel(hn_ref, wq_ref, wk_ref, wv_ref, wg_ref, wo_ref, q_ref, k_ref, v_ref, sg_ref, wo_bf16_ref,
                  w_bf16_ref, *, splits):
    @pl.when(pl.program_id(1) == 0)
    def _():
        for n, w_ref in enumerate((wq_ref, wk_ref, wv_ref, wg_ref)):
            w_bf16_ref[n] = w_ref[...].astype(BF16)

    wo_bf16_ref[...] = wo_ref[...].astype(BF16)

    for rows in _row_splits(hn_ref.shape[0], splits):
        hn = hn_ref[rows, :]
        dot = lambda n: jnp.dot(hn, w_bf16_ref[n], preferred_element_type=F32)
        q_ref[rows, :] = (dot(0) * (SB_HEAD_DIM ** -0.5 * LOG2_E)).astype(BF16)
        k_ref[rows, :] = dot(1).astype(BF16)
        v_ref[rows, :] = dot(2).astype(BF16)
        sg_ref[rows, :] = _silu(dot(3)).astype(BF16)


def _sb_in(hn, w_in, w_out, *, tm, tn, splits):
    tokens, d_model = hn.shape
    width = w_in.shape[1] // 4
    nj = width // tn
    ni = tokens // tm
    w_spec = lambda off: pl.BlockSpec((d_model, tn), lambda j, i, off=off: (0, j + off))
    out_spec = pl.BlockSpec((tm, tn), lambda j, i: (i, j))
    out_shape = jax.ShapeDtypeStruct((tokens, width), BF16)
    slab = w_out.shape[0] // (nj * ni)
    slab_spec = pl.BlockSpec((slab, w_out.shape[1]), lambda j, i: (j * ni + i, 0))
    return pl.pallas_call(
        functools.partial(_sb_in_kernel, splits=splits),
        grid=(nj, ni),
        in_specs=[
            pl.BlockSpec((tm, d_model), lambda j, i: (i, 0)),
            w_spec(0), w_spec(nj), w_spec(2 * nj), w_spec(3 * nj),
            slab_spec,
        ],
        out_specs=[out_spec] * 4 + [slab_spec],
        out_shape=[out_shape] * 4 + [jax.ShapeDtypeStruct(w_out.shape, BF16)],
        scratch_shapes=[pltpu.VMEM((4, d_model, tn), BF16)],
        compiler_params=_params(("parallel", "arbitrary")),
        name="sb_in_proj",
    )(hn, w_in, w_in, w_in, w_in, w_out)


def _sb_attn_kernel(q_ref, k_ref, v_ref, sg_ref, o_ref, carry_ref, acc_ref, *, tq, nh, nqb):
    row = lax.broadcasted_iota(jnp.int32, (tq, tq), 0)
    col = lax.broadcasted_iota(jnp.int32, (tq, tq), 1)
    suffix = (row > col).astype(BF16)
    causal = col < row
    lanes = [slice(h * SB_HEAD_DIM, (h + 1) * SB_HEAD_DIM) for h in range(nh)]
    q_rows = _row_splits(nqb * tq, nqb)
    q_blocks = [pl.program_id(2) * nqb + s for s in range(nqb)]

    def key_rows(kb):
        return pl.ds(pl.multiple_of(kb * tq, tq), tq)

    def logits(s, kb, h):
        return lax.dot_general(q_ref[q_rows[s], lanes[h]], k_ref[key_rows(kb), lanes[h]],
                               (((1,), (1,)), ((), ())), preferred_element_type=F32)

    def log_weights(z, masked):
        if masked:
            z = jnp.where(causal, z, NO_WEIGHT)
        log1p_term = jnp.log2(1.0 + jnp.exp2(-jnp.abs(z)))
        log_beta = jnp.minimum(z, 0.0) - log1p_term
        log_fail = log_beta - z
        tail = jnp.dot(log_fail.astype(BF16), suffix, preferred_element_type=F32)
        return log_beta + tail, jnp.sum(log_fail, axis=-1, keepdims=True)

    def weighted_values(kb, h, log_a, carry):
        a = jnp.exp2(log_a if carry is None else log_a + carry)
        return jnp.dot(a.astype(BF16), v_ref[key_rows(kb), lanes[h]], preferred_element_type=F32)

    units = []
    for s in range(nqb):
        prev = jnp.maximum(q_blocks[s] - 1, 0)
        no_prev = jnp.where(q_blocks[s] > 0, 0.0, NO_WEIGHT)
        for h in range(nh):
            units += [(s, h, q_blocks[s], True, None), (s, h, prev, False, no_prev)]
    zs, lws, outs = {}, {}, {}
    for t in range(len(units) + 2 * STAGE_SKEW):
        if t < len(units):
            s, h, kb, _, _ = units[t]
            zs[t] = logits(s, kb, h)
        u = t - STAGE_SKEW
        if 0 <= u < len(units):
            lws[u] = log_weights(zs.pop(u), units[u][3])
        u = t - 2 * STAGE_SKEW
        if 0 <= u < len(units):
            _, h, kb, masked, no_prev = units[u]
            carry = None if masked else lws[u - 1][1] + no_prev
            outs[u] = weighted_values(kb, h, lws[u][0], carry)

    def any_live(carries):
        top = carries[0]
        for carry in carries[1:]:
            top = jnp.maximum(top, carry)
        return jnp.max(top) >= ZERO_WEIGHT_LOG2

    for s in range(nqb):
        first = 2 * nh * s
        carries = [lws[first + 2 * h][1] + lws[first + 2 * h + 1][1] for h in range(nh)]
        for h in range(nh):
            carry_ref[s, h] = carries[h]
            acc_ref[s, h] = outs[first + 2 * h] + outs[first + 2 * h + 1]
        alive = any_live(carries)

        @pl.when(jnp.logical_and(q_blocks[s] >= 2, alive))
        def _(s=s, alive=alive):
            def cond(loop_state):
                kb, alive = loop_state
                return jnp.logical_and(kb >= 0, alive)

            def body(loop_state):
                kb, _ = loop_state
                carries = []
                for h in range(nh):
                    carry = carry_ref[s, h]
                    log_a, block_sum = log_weights(logits(s, kb, h), masked=False)
                    acc_ref[s, h] += weighted_values(kb, h, log_a, carry)
                    carries.append(carry + block_sum)
                    carry_ref[s, h] = carries[h]
                return kb - 1, any_live(carries)

            lax.while_loop(cond, body, (q_blocks[s] - 2, alive))

        for h in range(nh):
            o_ref[q_rows[s], lanes[h]] = (acc_ref[s, h] * sg_ref[q_rows[s], lanes[h]].astype(F32)).astype(BF16)


def _sb_attn(q, k, v, sg, *, batch, seq, tq, nh, nqb):
    tokens, width = q.shape
    heads = width // SB_HEAD_DIM
    steps = seq // (tq * nqb)
    grid = (batch, heads // nh, steps)
    q_spec = pl.BlockSpec((nqb * tq, nh * SB_HEAD_DIM), lambda b, h, i: (b * steps + i, h))
    kv_spec = pl.BlockSpec((seq, nh * SB_HEAD_DIM), lambda b, h, i: (b, h))
    return pl.pallas_call(
        functools.partial(_sb_attn_kernel, tq=tq, nh=nh, nqb=nqb),
        grid=grid,
        in_specs=[q_spec, kv_spec, kv_spec, q_spec],
        out_specs=q_spec,
        out_shape=jax.ShapeDtypeStruct((tokens, width), BF16),
        scratch_shapes=[pltpu.VMEM((nqb, nh, tq, 1), F32), pltpu.VMEM((nqb, nh, tq, SB_HEAD_DIM), F32)],
        compiler_params=_params(("parallel", "parallel", "arbitrary")),
        name="sb_attention",
    )(q, k, v, sg)


def _sb_out_kernel(h_ref, o_ref, wo_ref, g_ref, out_ref, *, splits):
    for rows in _row_splits(h_ref.shape[0], splits):
        h2 = h_ref[rows, :] + jnp.dot(o_ref[rows, :], wo_ref[...], preferred_element_type=F32)
        out_ref[rows, :] = _rms_normalize(h2, g_ref[...])


def _sb_out(h, og, w_out_bf16, final_g, *, tm, splits):
    tokens, d_model = h.shape
    width = og.shape[1]
    return pl.pallas_call(
        functools.partial(_sb_out_kernel, splits=splits),
        grid=(tokens // tm,),
        in_specs=[
            pl.BlockSpec((tm, d_model), lambda i: (i, 0)),
            pl.BlockSpec((tm, width), lambda i: (i, 0)),
            pl.BlockSpec((width, d_model), lambda i: (0, 0), pipeline_mode=pl.Buffered(1)),
            pl.BlockSpec((1, d_model), lambda i: (0, 0)),
        ],
        out_specs=pl.BlockSpec((tm, d_model), lambda i: (i, 0)),
        out_shape=jax.ShapeDtypeStruct((tokens, d_model), F32),
        compiler_params=_params(("parallel",)),
        name="sb_out_proj_norm",
    )(h, og, w_out_bf16, final_g)


def kernel(x, norm_g, a_w_in, a_v_norm_g, a_w_s, a_b_s, a_w_out, b_w_in, b_w_out, final_g):
    batch, seq, d_model = x.shape
    assert norm_g.shape[0] == 2 and a_w_in.shape[0] == 1 and b_w_in.shape[0] == 1
    x2 = x.reshape(batch * seq, d_model)
    ug, v, rinv, a_w_out_bf16 = _gmlp_in(x2, norm_g[0][None, :], a_w_in[0], a_w_out[0], tm=1024, tn=512)
    h1, hn1 = _gmlp_out(x2, v, ug, rinv, a_v_norm_g[0][None, :], a_w_s[0], a_b_s[0].T,
                        a_w_out_bf16, norm_g[1][None, :], tm=256, splits=1)
    q, k, v, sg, b_w_out_bf16 = _sb_in(hn1, b_w_in[0], b_w_out[0], tm=2048, tn=256, splits=2)
    og = _sb_attn(q, k, v, sg, batch=batch, seq=seq, tq=256, nh=8, nqb=2)
    out = _sb_out(h1, og, b_w_out_bf16, final_g[None, :], tm=1024, splits=4)
    return out.reshape(batch, seq, d_model)
```

```python
import functools

import jax
import jax.numpy as jnp
from jax import lax
from jax.experimental import pallas as pl
from jax.experimental.pallas import tpu as pltpu

EPS = 1e-6
CHUNK = 128
GMLP_GROUPS = 16
SB_HEAD_DIM = 128
GELU_C = 0.7978845608028654
LOG2_E = 1.4426950408889634
ZERO_WEIGHT_LOG2 = -160.0
NO_WEIGHT = -1e30
MATMUL_ROW_SPLITS = 8
STAGE_SKEW = 1

V7X_VMEM_LIMIT_BYTES = 60 * 1024 * 1024

BF16 = jnp.bfloat16
F32 = jnp.float32


def _gelu_tanh(x):
    return 0.5 * x * (1.0 + jnp.tanh(GELU_C * (x + 0.044715 * (x * x * x))))


def _silu(x):
    return 0.5 * x * (1.0 + jnp.tanh(0.5 * x))


def _rms_normalize(x_f32, gain_f32):
    ms = jnp.mean(x_f32 * x_f32, axis=-1, keepdims=True)
    return x_f32 * lax.rsqrt(ms + EPS) * gain_f32


def _row_splits(rows, parts):
    step = rows // parts
    return [slice(r * step, (r + 1) * step) for r in range(parts)]


def _params(semantics):
    return pltpu.CompilerParams(dimension_semantics=semantics,
                                vmem_limit_bytes=V7X_VMEM_LIMIT_BYTES)


def _gmlp_in_kernel(x_ref, g_ref, wu_ref, wv_ref, wg_ref, wo_ref, ug_ref, v_ref, rinv_ref, wo_bf16_ref,
                    xn_ref, ssq_ref, *, width):
    j = pl.program_id(1)
    wo_bf16_ref[...] = wo_ref[...].astype(BF16)

    def column_step(first):
        wu, wv, wg = (w_ref[...].astype(BF16) for w_ref in (wu_ref, wv_ref, wg_ref))
        for rows in _row_splits(xn_ref.shape[0], MATMUL_ROW_SPLITS):
            if first:
                xn = _rms_normalize(x_ref[rows, :], g_ref[...]).astype(BF16)
                xn_ref[rows, :] = xn
            else:
                xn = xn_ref[rows, :]
            v = _gelu_tanh(jnp.dot(xn, wv, preferred_element_type=F32))
            u = jnp.dot(xn, wu, preferred_element_type=F32)
            zg = jnp.dot(xn, wg, preferred_element_type=F32)
            v_ref[rows, :] = v.astype(BF16)
            ssq = jnp.sum(v * v, axis=-1, keepdims=True)
            ssq_ref[rows, :] = ssq if first else ssq_ref[rows, :] + ssq
            ug_ref[rows, :] = (_gelu_tanh(u) * _silu(zg)).astype(BF16)

    pl.when(j == 0)(functools.partial(column_step, True))
    pl.when(j > 0)(functools.partial(column_step, False))

    @pl.when(j == pl.num_programs(1) - 1)
    def _():
        rinv_ref[...] = lax.rsqrt(ssq_ref[...] * (1.0 / width) + EPS)


def _gmlp_in(x2, gain, w_in, w_out, *, tm, tn):
    tokens, d_model = x2.shape
    width = w_in.shape[1] // 3
    nj = width // tn
    grid = (tokens // tm, nj)
    w_spec = lambda off: pl.BlockSpec((d_model, tn), lambda i, j, off=off: (0, j + off))
    slab = w_out.shape[0] // (grid[0] * grid[1])
    slab_spec = pl.BlockSpec((slab, d_model), lambda i, j: (i * nj + j, 0))
    return pl.pallas_call(
        functools.partial(_gmlp_in_kernel, width=width),
        grid=grid,
        in_specs=[
            pl.BlockSpec((tm, d_model), lambda i, j: (i, 0)),
            pl.BlockSpec((1, d_model), lambda i, j: (0, 0)),
            w_spec(0), w_spec(nj), w_spec(2 * nj),
            slab_spec,
        ],
        out_specs=[
            pl.BlockSpec((tm, tn), lambda i, j: (i, j)),
            pl.BlockSpec((tm, tn), lambda i, j: (i, j)),
            pl.BlockSpec((tm, 1), lambda i, j: (i, 0)),
            slab_spec,
        ],
        out_shape=[
            jax.ShapeDtypeStruct((tokens, width), BF16),
            jax.ShapeDtypeStruct((tokens, width), BF16),
            jax.ShapeDtypeStruct((tokens, 1), F32),
            jax.ShapeDtypeStruct(w_out.shape, BF16),
        ],
        scratch_shapes=[pltpu.VMEM((tm, d_model), BF16), pltpu.VMEM((tm, 1), F32)],
        compiler_params=_params(("parallel", "arbitrary")),
        name="gmlp_in_proj",
    )(x2, gain, w_in, w_in, w_in, w_out)


def _gmlp_out_kernel(x_ref, v_ref, ug_ref, rinv_ref, vg_ref, ws_ref, bs_ref, wo_ref, ng_ref,
                     h_ref, hn_ref, y_ref, *, tm, gd, splits):
    row = lax.broadcasted_iota(jnp.int32, (CHUNK, CHUNK), 0)
    col = lax.broadcasted_iota(jnp.int32, (CHUNK, CHUNK), 1)
    ws = [jnp.where(row >= col, ws_ref[g], 0.0).astype(BF16) for g in range(GMLP_GROUPS)]

    def gate(part):
        for c in range(part.start // CHUNK, part.stop // CHUNK):
            rows = slice(c * CHUNK, (c + 1) * CHUNK)
            rinv = rinv_ref[rows, :]
            for g in range(GMLP_GROUPS):
                cols = slice(g * gd, (g + 1) * gd)
                vn = (v_ref[rows, cols].astype(F32) * rinv * vg_ref[:, cols]).astype(BF16)
                mixed = jnp.dot(ws[g], vn, preferred_element_type=F32) + bs_ref[:, g:g + 1]
                y_ref[rows, cols] = (ug_ref[rows, cols].astype(F32) * mixed).astype(BF16)

    def project(part):
        h = x_ref[part, :] + jnp.dot(y_ref[part, :], wo_ref[...], preferred_element_type=F32)
        h_ref[part, :] = h
        hn_ref[part, :] = _rms_normalize(h, ng_ref[...]).astype(BF16)

    for part in _row_splits(tm, splits):
        gate(part)
        project(part)


def _gmlp_out(x2, v, ug, rinv, v_gain, w_s, b_s_t, w_out_bf16, next_gain, *, tm, splits):
    tokens, d_model = x2.shape
    width = v.shape[1]
    gd = width // GMLP_GROUPS
    row_tile = lambda cols: pl.BlockSpec((tm, cols), lambda i: (i, 0))
    whole = lambda shape: pl.BlockSpec(shape, lambda i: (0,) * len(shape), pipeline_mode=pl.Buffered(1))
    return pl.pallas_call(
        functools.partial(_gmlp_out_kernel, tm=tm, gd=gd, splits=splits),
        grid=(tokens // tm,),
        in_specs=[
            row_tile(d_model), row_tile(width), row_tile(width), row_tile(1),
            whole((1, width)), whole((GMLP_GROUPS, CHUNK, CHUNK)), whole((CHUNK, GMLP_GROUPS)),
            whole((width, d_model)), whole((1, d_model)),
        ],
        out_specs=[row_tile(d_model)] * 2,
        out_shape=[jax.ShapeDtypeStruct((tokens, d_model), F32),
                   jax.ShapeDtypeStruct((tokens, d_model), BF16)],
        scratch_shapes=[pltpu.VMEM((tm, width), BF16)],
        compiler_params=pltpu.CompilerParams(
            dimension_semantics=("parallel",), vmem_limit_bytes=V7X_VMEM_LIMIT_BYTES,
            allow_input_fusion=[False, False, False, False, True, False, True, False, True]),
        name="gmlp_mix_out_proj",
    )(x2, v, ug, rinv, v_gain, w_s, b_s_t, w_out_bf16, next_gain)


def _sb_in_kernel(hn_ref, wq_ref, wk_ref, wv_ref, wg_ref, wo_ref, q_ref, k_ref, v_ref, sg_ref, wo_bf16_ref,
                  w_bf16_ref, *, splits):
    @pl.when(pl.program_id(1) == 0)
    def _():
        for n, w_ref in enumerate((wq_ref, wk_ref, wv_ref, wg_ref)):
            w_bf16_ref[n] = w_ref[...].astype(BF16)

    wo_bf16_ref[...] = wo_ref[...].astype(BF16)

    for rows in _row_splits(hn_ref.shape[0], splits):
        hn = hn_ref[rows, :]
        dot = lambda n: jnp.dot(hn, w_bf16_ref[n], preferred_element_type=F32)
        q_ref[rows, :] = (dot(0) * (SB_HEAD_DIM ** -0.5 * LOG2_E)).astype(BF16)
        k_ref[rows, :] = dot(1).astype(BF16)
        v_ref[rows, :] = dot(2).astype(BF16)
        sg_ref[rows, :] = _silu(dot(3)).astype(BF16)


def _sb_in(hn, w_in, w_out, *, tm, tn, splits):
    tokens, d_model = hn.shape
    width = w_in.shape[1] // 4
    nj = width // tn
    ni = tokens // tm
    w_spec = lambda off: pl.BlockSpec((d_model, tn), lambda j, i, off=off: (0, j + off))
    out_spec = pl.BlockSpec((tm, tn), lambda j, i: (i, j))
    out_shape = jax.ShapeDtypeStruct((tokens, width), BF16)
    slab = w_out.shape[0] // (nj * ni)
    slab_spec = pl.BlockSpec((slab, w_out.shape[1]), lambda j, i: (j * ni + i, 0))
    return pl.pallas_call(
        functools.partial(_sb_in_kernel, splits=splits),
        grid=(nj, ni),
        in_specs=[
            pl.BlockSpec((tm, d_model), lambda j, i: (i, 0)),
            w_spec(0), w_spec(nj), w_spec(2 * nj), w_spec(3 * nj),
            slab_spec,
        ],
        out_specs=[out_spec] * 4 + [slab_spec],
        out_shape=[out_shape] * 4 + [jax.ShapeDtypeStruct(w_out.shape, BF16)],
        scratch_shapes=[pltpu.VMEM((4, d_model, tn), BF16)],
        compiler_params=_params(("parallel", "arbitrary")),
        name="sb_in_proj",
    )(hn, w_in, w_in, w_in, w_in, w_out)


def _sb_attn_kernel(q_ref, k_ref, v_ref, sg_ref, o_ref, carry_ref, acc_ref, *, tq, nh, nqb):
    row = lax.broadcasted_iota(jnp.int32, (tq, tq), 0)
    col = lax.broadcasted_iota(jnp.int32, (tq, tq), 1)
    suffix = (row > col).astype(BF16)
    causal = col < row
    lanes = [slice(h * SB_HEAD_DIM, (h + 1) * SB_HEAD_DIM) for h in range(nh)]
    q_rows = _row_splits(nqb * tq, nqb)
    q_blocks = [pl.program_id(2) * nqb + s for s in range(nqb)]

    def key_rows(kb):
        return pl.ds(pl.multiple_of(kb * tq, tq), tq)

    def logits(s, kb, h):
        return lax.dot_general(q_ref[q_rows[s], lanes[h]], k_ref[key_rows(kb), lanes[h]],
                               (((1,), (1,)), ((), ())), preferred_element_type=F32)

    def log_weights(z, masked):
        if masked:
            z = jnp.where(causal, z, NO_WEIGHT)
        log1p_term = jnp.log2(1.0 + jnp.exp2(-jnp.abs(z)))
        log_beta = jnp.minimum(z, 0.0) - log1p_term
        log_fail = log_beta - z
        tail = jnp.dot(log_fail.astype(BF16), suffix, preferred_element_type=F32)
        return log_beta + tail, jnp.sum(log_fail, axis=-1, keepdims=True)

    def weighted_values(kb, h, log_a, carry):
        a = jnp.exp2(log_a if carry is None else log_a + carry)
        return jnp.dot(a.astype(BF16), v_ref[key_rows(kb), lanes[h]], preferred_element_type=F32)

    units = []
    for s in range(nqb):
        prev = jnp.maximum(q_blocks[s] - 1, 0)
        no_prev = jnp.where(q_blocks[s] > 0, 0.0, NO_WEIGHT)
        for h in range(nh):
            units += [(s, h, q_blocks[s], True, None), (s, h, prev, False, no_prev)]
    zs, lws, outs = {}, {}, {}
    for t in range(len(units) + 2 * STAGE_SKEW):
        if t < len(units):
            s, h, kb, _, _ = units[t]
            zs[t] = logits(s, kb, h)
        u = t - STAGE_SKEW
        if 0 <= u < len(units):
            lws[u] = log_weights(zs.pop(u), units[u][3])
        u = t - 2 * STAGE_SKEW
        if 0 <= u < len(units):
            _, h, kb, masked, no_prev = units[u]
            carry = None if masked else lws[u - 1][1] + no_prev
            outs[u] = weighted_values(kb, h, lws[u][0], carry)

    def any_live(carries):
        top = carries[0]
        for carry in carries[1:]:
            top = jnp.maximum(top, carry)
        return jnp.max(top) >= ZERO_WEIGHT_LOG2

    for s in range(nqb):
        first = 2 * nh * s
        carries = [lws[first + 2 * h][1] + lws[first + 2 * h + 1][1] for h in range(nh)]
        for h in range(nh):
            carry_ref[s, h] = carries[h]
            acc_ref[s, h] = outs[first + 2 * h] + outs[first + 2 * h + 1]
        alive = any_live(carries)

        @pl.when(jnp.logical_and(q_blocks[s] >= 2, alive))
        def _(s=s, alive=alive):
            def cond(loop_state):
                kb, alive = loop_state
                return jnp.logical_and(kb >= 0, alive)

            def body(loop_state):
                kb, _ = loop_state
                carries = []
                for h in range(nh):
                    carry = carry_ref[s, h]
                    log_a, block_sum = log_weights(logits(s, kb, h), masked=False)
                    acc_ref[s, h] += weighted_values(kb, h, log_a, carry)
                    carries.append(carry + block_sum)
                    carry_ref[s, h] = carries[h]
                return kb - 1, any_live(carries)

            lax.while_loop(cond, body, (q_blocks[s] - 2, alive))

        for h in range(nh):
            o_ref[q_rows[s], lanes[h]] = (acc_ref[s, h] * sg_ref[q_rows[s], lanes[h]].astype(F32)).astype(BF16)


def _sb_attn(q, k, v, sg, *, batch, seq, tq, nh, nqb):
    tokens, width = q.shape
    heads = width // SB_HEAD_DIM
    steps = seq // (tq * nqb)
    grid = (batch, heads // nh, steps)
    q_spec = pl.BlockSpec((nqb * tq, nh * SB_HEAD_DIM), lambda b, h, i: (b * steps + i, h))
    kv_spec = pl.BlockSpec((seq, nh * SB_HEAD_DIM), lambda b, h, i: (b, h))
    return pl.pallas_call(
        functools.partial(_sb_attn_kernel, tq=tq, nh=nh, nqb=nqb),
        grid=grid,
        in_specs=[q_spec, kv_spec, kv_spec, q_spec],
        out_specs=q_spec,
        out_shape=jax.ShapeDtypeStruct((tokens, width), BF16),
        scratch_shapes=[pltpu.VMEM((nqb, nh, tq, 1), F32), pltpu.VMEM((nqb, nh, tq, SB_HEAD_DIM), F32)],
        compiler_params=_params(("parallel", "parallel", "arbitrary")),
        name="sb_attention",
    )(q, k, v, sg)


def _sb_out_kernel(h_ref, o_ref, wo_ref, g_ref, out_ref, *, splits):
    for rows in _row_splits(h_ref.shape[0], splits):
        h2 = h_ref[rows, :] + jnp.dot(o_ref[rows, :], wo_ref[...], preferred_element_type=F32)
        out_ref[rows, :] = _rms_normalize(h2, g_ref[...])


def _sb_out(h, og, w_out_bf16, final_g, *, tm, splits):
    tokens, d_model = h.shape
    width = og.shape[1]
    return pl.pallas_call(
        functools.partial(_sb_out_kernel, splits=splits),
        grid=(tokens // tm,),
        in_specs=[
            pl.BlockSpec((tm, d_model), lambda i: (i, 0)),
            pl.BlockSpec((tm, width), lambda i: (i, 0)),
            pl.BlockSpec((width, d_model), lambda i: (0, 0), pipeline_mode=pl.Buffered(1)),
            pl.BlockSpec((1, d_model), lambda i: (0, 0)),
        ],
        out_specs=pl.BlockSpec((tm, d_model), lambda i: (i, 0)),
        out_shape=jax.ShapeDtypeStruct((tokens, d_model), F32),
        compiler_params=_params(("parallel",)),
        name="sb_out_proj_norm",
    )(h, og, w_out_bf16, final_g)


def kernel(x, norm_g, a_w_in, a_v_norm_g, a_w_s, a_b_s, a_w_out, b_w_in, b_w_out, final_g):
    batch, seq, d_model = x.shape
    assert norm_g.shape[0] == 2 and a_w_in.shape[0] == 1 and b_w_in.shape[0] == 1
    x2 = x.reshape(batch * seq, d_model)
    ug, v, rinv, a_w_out_bf16 = _gmlp_in(x2, norm_g[0][None, :], a_w_in[0], a_w_out[0], tm=1024, tn=512)
    h1, hn1 = _gmlp_out(x2, v, ug, rinv, a_v_norm_g[0][None, :], a_w_s[0], a_b_s[0].T,
                        a_w_out_bf16, norm_g[1][None, :], tm=512, splits=2)
    q, k, v, sg, b_w_out_bf16 = _sb_in(hn1, b_w_in[0], b_w_out[0], tm=2048, tn=256, splits=2)
    og = _sb_attn(q, k, v, sg, batch=batch, seq=seq, tq=256, nh=8, nqb=2)
    out = _sb_out(h1, og, b_w_out_bf16, final_g[None, :], tm=1024, splits=4)
    return out.reshape(batch, seq, d_model)
```
